```python
import jax, jax.numpy as jnp
from jax import lax
import numpy as np

D_MODEL = 1024
BATCH = 2
SEQ = 8192
DEPTH = 4
DEC_BATCH = 32
DEC_SEQ = 1
PAST_LEN = 8192
PAGE_SIZE = 128

N_A_LAYERS = DEPTH // 2
N_B_LAYERS = DEPTH - N_A_LAYERS
A_HEADS = 8
A_DK = D_MODEL // A_HEADS
A_DV = D_MODEL // A_HEADS
A_WIDTH = A_HEADS * A_DV
A_CHUNK = 64
B_HEADS = 16
B_DH = D_MODEL // B_HEADS
B_WIDTH = B_HEADS * B_DH
Q_BLOCK = 128
RMS_EPS = 1e-6
FGATE_BIAS_INIT = 3.0

kernel_name = 'yoco_hgrn2_fox_decoder_step'


def rms_norm(x, gain):
    xf = x.astype(jnp.float32)
    xf = xf * lax.rsqrt(jnp.mean(xf * xf, axis=-1, keepdims=True) + RMS_EPS)
    return (xf * gain.astype(jnp.float32)).astype(x.dtype)


def hgrn_lower_bounds(lb_logits):
    p = jax.nn.softmax(lb_logits.astype(jnp.float32), axis=0)
    c = jnp.cumsum(p, axis=0)
    return c - c[:1]


def gated_recurrence(q, k, v, logf, s0, chunk):
    bsz, t, h, _ = q.shape
    n = t // chunk

    def to_chunks(a):
        return a.reshape(bsz, n, chunk, h, a.shape[-1]).transpose(1, 0, 3, 2, 4)

    causal = jnp.tril(jnp.ones((chunk, chunk), dtype=bool))[:, :, None]

    def step(s, inp):
        qc, kc, vc, gc = inp
        cum = jnp.cumsum(gc, axis=2)
        o_inter = jnp.einsum('bhtk,bhkv->bhtv', qc * jnp.exp(cum), s)
        diff = cum[:, :, :, None, :] - cum[:, :, None, :, :]
        decay = jnp.exp(jnp.where(causal, diff, -jnp.inf))
        att = jnp.einsum('bhtk,bhsk,bhtsk->bhts', qc, kc, decay)
        o = o_inter + jnp.einsum('bhts,bhsv->bhtv', att, vc)
        last = cum[:, :, -1:, :]
        s_new = jnp.exp(last[:, :, 0, :])[..., None] * s + jnp.einsum(
            'bhsk,bhsv->bhkv', kc * jnp.exp(last - cum), vc)
        return s_new, o

    s_t, o = lax.scan(step, s0, (to_chunks(q), to_chunks(k), to_chunks(v), to_chunks(logf)))
    return o.transpose(1, 0, 3, 2, 4).reshape(bsz, t, h, -1), s_t


def hgrn_layer(x, s0, norm_g, w_in, g_norm, w_o, lb):
    bsz, t, _ = x.shape
    chunk = A_CHUNK if t % A_CHUNK == 0 else t
    hk = A_HEADS * A_DK
    proj = rms_norm(x, norm_g) @ w_in
    q, fz, i, gate = jnp.split(proj, [hk, 2 * hk, 2 * hk + A_WIDTH], axis=-1)
    heads_k = (bsz, t, A_HEADS, A_DK)
    q = jax.nn.silu(q.astype(jnp.float32)).reshape(heads_k)
    fz = fz.astype(jnp.float32).reshape(heads_k)
    lb = lb.reshape(A_HEADS, A_DK)
    logf = jnp.logaddexp(jnp.log(lb), jnp.log1p(-lb) + jax.nn.log_sigmoid(fz))
    k = (1.0 - lb) * jax.nn.sigmoid(-fz)
    v = i.astype(jnp.float32).reshape(bsz, t, A_HEADS, A_DV)
    o, s_t = gated_recurrence(q, k, v, logf, s0.astype(jnp.float32), chunk)
    o = rms_norm(o, g_norm).reshape(bsz, t, A_WIDTH).astype(x.dtype) * jax.nn.silu(gate)
    return x + o @ w_o, s_t.astype(s0.dtype)


def shared_kv(h, norm_kv, w_kv, k_norm, w_fgate, b_fgate):
    bsz, t, _ = h.shape
    u = rms_norm(h, norm_kv)
    k, v = jnp.split(u @ w_kv, 2, axis=-1)
    k = rms_norm(k.reshape(bsz, t, B_HEADS, B_DH), k_norm)
    v = v.reshape(bsz, t, B_HEADS, B_DH)
    logf = jax.nn.log_sigmoid((u @ w_fgate + b_fgate).astype(jnp.float32))
    return k, v, logf


def forgetting_attention(q, k_all, v_all, c_all):
    bsz, tq = q.shape[:2]
    length = k_all.shape[1]
    q_start = length - tq
    q_block = Q_BLOCK if tq % Q_BLOCK == 0 else tq
    n_blk = tq // q_block
    c_t = jnp.transpose(c_all, (0, 2, 1))
    key_pos = jnp.arange(length)
    scale = B_DH ** -0.5
    qb = q.reshape(bsz, n_blk, q_block, B_HEADS, B_DH).transpose(1, 0, 2, 3, 4)
    starts = q_start + jnp.arange(n_blk) * q_block

    def one_block(args):
        qi, start = args
        s = jnp.einsum('bqhd,bkhd->bhqk', qi, k_all).astype(jnp.float32) * scale
        c_q = lax.dynamic_slice_in_dim(c_t, start, q_block, axis=2)
        q_pos = start + jnp.arange(q_block)
        mask = key_pos[None, :] <= q_pos[:, None]
        logits = jnp.where(mask, s + (c_q[..., :, None] - c_t[..., None, :]), -jnp.inf)
        p = jax.nn.softmax(logits, axis=-1)
        return jnp.einsum('bhqk,bkhd->bqhd', p.astype(v_all.dtype), v_all)

    out = lax.map(one_block, (qb, starts))
    return out.transpose(1, 0, 2, 3, 4).reshape(bsz, tq, B_HEADS, B_DH)


def fox_layer(x, norm_g, w_in, q_norm, w_o, k_all, v_all, c_all):
    bsz, t, _ = x.shape
    q, gate = jnp.split(rms_norm(x, norm_g) @ w_in, 2, axis=-1)
    q = rms_norm(q.reshape(bsz, t, B_HEADS, B_DH), q_norm)
    o = forgetting_attention(q, k_all, v_all, c_all)
    o = o.reshape(bsz, t, B_WIDTH).astype(x.dtype) * jax.nn.silu(gate)
    return x + o @ w_o


def trunk(x, s0, past_k, past_v, past_logf, norm_in, w_in_a, g_norm_a, w_o_a, lbs,
          norm_kv, w_kv, k_norm, w_fgate, b_fgate, w_in_b, q_norm_b, w_o_b):
    h = x
    new_s = []
    for layer in range(DEPTH):
        if layer < N_A_LAYERS:
            h, s = hgrn_layer(h, s0[layer], norm_in[layer], w_in_a[layer], g_norm_a[layer],
                              w_o_a[layer], lbs[layer])
            new_s.append(s)
            if layer == N_A_LAYERS - 1:
                k_new, v_new, lf_new = shared_kv(h, norm_kv, w_kv, k_norm, w_fgate, b_fgate)
                k_new = k_new.astype(past_k.dtype)
                v_new = v_new.astype(past_v.dtype)
                k_all = jnp.concatenate([past_k, k_new], axis=1)
                v_all = jnp.concatenate([past_v, v_new], axis=1)
                c_all = jnp.cumsum(jnp.concatenate(
                    [past_logf.astype(jnp.float32), lf_new], axis=1), axis=1)
                lf_new = lf_new.astype(past_logf.dtype)
        else:
            j = layer - N_A_LAYERS
            h = fox_layer(h, norm_in[layer], w_in_b[j], q_norm_b[j], w_o_b[j], k_all, v_all, c_all)
    return h, jnp.stack(new_s), k_new, v_new, lf_new


def setup_inputs(seed: int = 0) -> dict:
    key = jax.random.key(seed)
    ks = jax.random.split(key, 20)
    f32 = jnp.float32
    n_pages = PAST_LEN // PAGE_SIZE
    n_used = DEC_BATCH * n_pages
    n_pool = n_used + (n_used + 3) // 4
    d_scale = D_MODEL ** -0.5

    def normal(k, shape):
        return jax.random.normal(k, shape, f32)

    def gain(k, shape):
        return 1.0 + 0.02 * normal(k, shape)

    page_table = jax.random.permutation(ks[0], n_pool)[:n_used].reshape(
        DEC_BATCH, n_pages).astype(jnp.int32)
    return {
        'x_prompt': normal(ks[1], (BATCH, SEQ, D_MODEL)),
        'x_sample': normal(ks[2], (DEC_BATCH, DEC_SEQ, D_MODEL)),
        'state_hgrn': 0.3 * normal(ks[3], (N_A_LAYERS, DEC_BATCH, A_HEADS, A_DK, A_DV)),
        'cache_k': normal(ks[4], (n_pool, PAGE_SIZE, B_HEADS, B_DH)),
        'cache_v': normal(ks[5], (n_pool, PAGE_SIZE, B_HEADS, B_DH)),
        'cache_logf': jax.nn.log_sigmoid(FGATE_BIAS_INIT + 0.5 * normal(ks[6], (n_pool, PAGE_SIZE, B_HEADS))),
        'page_table': page_table,
        'norm_in': gain(ks[7], (DEPTH, D_MODEL)),
        'w_in_a': d_scale * normal(ks[8], (N_A_LAYERS, D_MODEL, 2 * A_HEADS * A_DK + 2 * A_WIDTH)),
        'lb_logits': 0.5 * normal(ks[9], (N_A_LAYERS, A_HEADS * A_DK)),
        'g_norm_a': gain(ks[10], (N_A_LAYERS, A_DV)),
        'w_o_a': A_WIDTH ** -0.5 * normal(ks[11], (N_A_LAYERS, A_WIDTH, D_MODEL)),
        'norm_kv': gain(ks[12], (D_MODEL,)),
        'w_kv': d_scale * normal(ks[13], (D_MODEL, 2 * B_WIDTH)),
        'k_norm': gain(ks[14], (B_DH,)),
        'w_fgate': 0.5 * d_scale * normal(ks[15], (D_MODEL, B_HEADS)),
        'b_fgate': FGATE_BIAS_INIT + 0.1 * normal(ks[16], (B_HEADS,)),
        'w_in_b': d_scale * normal(ks[17], (N_B_LAYERS, D_MODEL, 2 * B_WIDTH)),
        'q_norm_b': gain(ks[18], (N_B_LAYERS, B_DH)),
        'w_o_b': B_WIDTH ** -0.5 * normal(ks[19], (N_B_LAYERS, B_WIDTH, D_MODEL)),
    }


def reference(x_prompt, x_sample, state_hgrn, cache_k, cache_v, cache_logf, page_table,
              norm_in, w_in_a, lb_logits, g_norm_a, w_o_a,
              norm_kv, w_kv, k_norm, w_fgate, b_fgate,
              w_in_b, q_norm_b, w_o_b):
    lbs = hgrn_lower_bounds(lb_logits)
    bp = x_prompt.shape[0]
    zero_state = jnp.zeros((N_A_LAYERS, bp, A_HEADS, A_DK, A_DV), state_hgrn.dtype)
    empty_k = jnp.zeros((bp, 0, B_HEADS, B_DH), cache_k.dtype)
    empty_v = jnp.zeros((bp, 0, B_HEADS, B_DH), cache_v.dtype)
    empty_lf = jnp.zeros((bp, 0, B_HEADS), cache_logf.dtype)
    y_prompt, s_prompt, k_prompt, v_prompt, lf_prompt = trunk(
        x_prompt, zero_state, empty_k, empty_v, empty_lf, norm_in, w_in_a, g_norm_a, w_o_a, lbs,
        norm_kv, w_kv, k_norm, w_fgate, b_fgate, w_in_b, q_norm_b, w_o_b)
    db, n_pages = page_table.shape
    past_len = n_pages * PAGE_SIZE
    past_k = cache_k[page_table].reshape(db, past_len, B_HEADS, B_DH)
    past_v = cache_v[page_table].reshape(db, past_len, B_HEADS, B_DH)
    past_lf = cache_logf[page_table].reshape(db, past_len, B_HEADS)
    y_sample, s_sample, k_sample, v_sample, lf_sample = trunk(
        x_sample, state_hgrn, past_k, past_v, past_lf, norm_in, w_in_a, g_norm_a, w_o_a, lbs,
        norm_kv, w_kv, k_norm, w_fgate, b_fgate, w_in_b, q_norm_b, w_o_b)
    return (y_prompt, y_sample, s_prompt, s_sample, k_prompt, k_sample, v_prompt, v_sample, lf_prompt, lf_sample)
```

```python
import functools

import jax
import jax.numpy as jnp
from jax import lax
from jax.experimental import pallas as pl
from jax.experimental.pallas import tpu as pltpu

F32 = jnp.float32
BF16 = jnp.bfloat16

D_MODEL = 1024
A_HEADS = 8
A_DK = 128
A_DV = 128
B_HEADS = 16
B_DH = 64
RMS_EPS = 1e-6
ATTN_SCALE = B_DH ** -0.5

LANES = 128
REC_CHUNK = 128
VMEM_LIMIT = 48 * 1024 * 1024

_NT = (((1,), (1,)), ((), ()))
_TN = (((0,), (0,)), ((), ()))


def _cparams(sem):
    return pltpu.CompilerParams(dimension_semantics=sem, vmem_limit_bytes=VMEM_LIMIT)


def _dot(a, b):
    return jnp.dot(a, b, preferred_element_type=F32)


def _rms_rows(x, g):
    ms = jnp.mean(x * x, axis=-1, keepdims=True)
    return x * lax.rsqrt(ms + RMS_EPS) * g


def _split3(x):
    hi = x.astype(BF16)
    r1 = x - hi.astype(F32)
    mid = r1.astype(BF16)
    lo = (r1 - mid.astype(F32)).astype(BF16)
    return hi, mid, lo


def _dot3(x, sel):
    hi, mid, lo = _split3(x)
    return _dot(hi, sel) + _dot(mid, sel) + _dot(lo, sel)


def _sel_dot3(sel, x):
    hi, mid, lo = _split3(x)
    return _dot(sel, hi) + _dot(sel, mid) + _dot(sel, lo)


def _sigmoid(x):
    return 1.0 / (1.0 + jnp.exp(-x))


def _head_rms(x, ind, ind_t, gain):
    ssum = _dot3(x * x, ind)
    r = lax.rsqrt(ssum * (1.0 / B_DH) + RMS_EPS)
    return x * _dot3(r, ind_t) * gain


def _hgrn_proj_kernel(layer, x_ref, g_ref, w_ref, lbl_ref, q_ref, k_ref, lf_ref, v_ref, sg_ref):
    xn = _rms_rows(x_ref[...], g_ref[...]).astype(BF16)
    lbl = lbl_ref[...]
    e = jnp.exp(lbl - jnp.max(lbl, axis=0, keepdims=True))
    p = e / jnp.sum(e, axis=0, keepdims=True)
    csum = p[0:1, :]
    for i in range(1, layer + 1):
        csum = csum + p[i:i + 1, :]
    lb = csum - p[0:1, :]
    log_lb = jnp.log(lb)
    log_1m_lb = jnp.log1p(-lb)
    hk = A_HEADS * A_DK

    pq = _dot(xn, w_ref[:, 0:hk])
    q_ref[...] = pq * _sigmoid(pq)

    fz = _dot(xn, w_ref[:, hk:2 * hk])
    ez = jnp.exp(-jnp.abs(fz))
    log_sig = jnp.minimum(fz, 0.0) - jnp.log1p(ez)
    sig_neg = jnp.where(fz >= 0.0, ez, 1.0) / (1.0 + ez)
    b = log_1m_lb + log_sig
    mx = jnp.maximum(log_lb, b)
    lf_ref[...] = mx + jnp.log1p(jnp.exp(-jnp.abs(log_lb - b)))
    k_ref[...] = (1.0 - lb) * sig_neg

    v_ref[...] = _dot(xn, w_ref[:, 2 * hk:2 * hk + A_HEADS * A_DV])
    pg = _dot(xn, w_ref[:, 2 * hk + A_HEADS * A_DV:])
    sg_ref[...] = pg * _sigmoid(pg)


def _hgrn_proj(x, g, w_bf, lb_logits, layer, tm):
    m = x.shape[0]
    n_in = w_bf.shape[1]
    row = pl.BlockSpec((tm, D_MODEL), lambda i: (i, 0))
    full = lambda shape: pl.BlockSpec(shape, lambda i: (0, 0))
    out = jax.ShapeDtypeStruct((m, D_MODEL), F32)
    return pl.pallas_call(
        functools.partial(_hgrn_proj_kernel, layer),
        grid=(m // tm,),
        in_specs=[row, full((1, D_MODEL)), full((D_MODEL, n_in)), full(lb_logits.shape)],
        out_specs=[row] * 5,
        out_shape=[out] * 5,
        compiler_params=_cparams(("parallel",)),
        name="hgrn_proj",
    )(x, g.reshape(1, D_MODEL), w_bf, lb_logits)


def _rec_chunk(q, k, v, lf, s_t):
    c = q.shape[0]
    row = lax.broadcasted_iota(jnp.int32, (c, LANES), 0)
    xor = lax.broadcasted_iota(jnp.int32, (c, c), 0) ^ lax.broadcasted_iota(jnp.int32, (c, c), 1)
    att = jnp.where(xor == 0, lax.dot_general(q.astype(BF16), k.astype(BF16), _NT,
                                              preferred_element_type=F32), 0.0)
    pre = lf
    tot = lf
    m = 1
    while m < c:
        right = (row & m) != 0
        e = jnp.exp(jnp.where(right, pre, tot - pre))
        qt = jnp.where(right, q * e, 0.0).astype(BF16)
        kt = jnp.where(right, 0.0, k * e).astype(BF16)
        a = lax.dot_general(qt, kt, _NT, preferred_element_type=F32)
        att = att + jnp.where(xor < 2 * m, a, 0.0)
        down = pltpu.roll(tot, m, 0)
        up = pltpu.roll(tot, c - m, 0)
        pre = pre + jnp.where(right, down, 0.0)
        tot = tot + jnp.where(right, down, up)
        m *= 2
    vb = v.astype(BF16)
    qg = (q * jnp.exp(pre)).astype(BF16)
    kh = (k * jnp.exp(tot - pre)).astype(BF16)
    o = lax.dot_general(qg, s_t.astype(BF16), _NT, preferred_element_type=F32)
    o = o + _dot(att.astype(BF16), vb)
    s_new = s_t * jnp.exp(tot[0:1, :]) + lax.dot_general(vb, kh, _TN, preferred_element_type=F32)
    return o, s_new


def _hgrn_rec_kernel(n_chunks, q_ref, k_ref, lf_ref, v_ref, s0_ref, gn_ref, o_ref, s_out_ref, st_ref):
    t = pl.program_id(2)

    @pl.when(t == 0)
    def _():
        st_ref[...] = s0_ref[0, 0].T

    def body(ci, carry):
        sl = pl.ds(pl.multiple_of(ci * REC_CHUNK, REC_CHUNK), REC_CHUNK)
        o, s_new = _rec_chunk(q_ref[sl, :], k_ref[sl, :], v_ref[sl, :], lf_ref[sl, :], st_ref[...])
        st_ref[...] = s_new
        o_ref[sl, :] = _rms_rows(o, gn_ref[...])
        return carry

    lax.fori_loop(0, n_chunks, body, 0)

    @pl.when(t == pl.num_programs(2) - 1)
    def _():
        s_out_ref[0, 0] = st_ref[...].T


def _hgrn_rec(q, k, lf, v, s0, g_norm, bsz, seq, tb):
    n_t = seq // tb
    blk = pl.BlockSpec((tb, A_DK), lambda b, h, t: (b * n_t + t, h))
    st = pl.BlockSpec((1, 1, A_DK, A_DV), lambda b, h, t: (b, h, 0, 0))
    return pl.pallas_call(
        functools.partial(_hgrn_rec_kernel, tb // REC_CHUNK),
        grid=(bsz, A_HEADS, n_t),
        in_specs=[blk, blk, blk, blk, st, pl.BlockSpec((1, A_DV), lambda b, h, t: (0, 0))],
        out_specs=[blk, st],
        out_shape=[jax.ShapeDtypeStruct(q.shape, F32), jax.ShapeDtypeStruct(s0.shape, F32)],
        scratch_shapes=[pltpu.VMEM((A_DV, A_DK), F32)],
        compiler_params=_cparams(("parallel", "parallel", "arbitrary")),
        name="hgrn_rec",
    )(q, k, lf, v, s0, g_norm.reshape(1, A_DV))


def _hgrn_step_kernel(nb, qt_ref, kt_ref, lft_ref, v_ref, s_ref, gn_ref, o_ref, s_out_ref):
    qt = qt_ref[0]
    kt = kt_ref[0]
    ft = jnp.exp(lft_ref[0])
    for b in range(nb):
        col = lambda a: jnp.broadcast_to(a[:, b:b + 1], (A_DK, A_DV))
        s_new = col(ft) * s_ref[b, 0] + col(kt) * v_ref[b:b + 1, :]
        s_out_ref[b, 0] = s_new
        o = jnp.sum(col(qt) * s_new, axis=0, keepdims=True)
        o_ref[b:b + 1, :] = _rms_rows(o, gn_ref[...])


def _hgrn_step(q, k, lf, v, s, g_norm):
    nb = q.shape[0]
    tr = lambda a: a.reshape(nb, A_HEADS, A_DK).transpose(1, 2, 0)
    colspec = pl.BlockSpec((1, A_DK, nb), lambda h: (h, 0, 0))
    vspec = pl.BlockSpec((nb, A_DV), lambda h: (0, h))
    sspec = pl.BlockSpec((nb, 1, A_DK, A_DV), lambda h: (0, h, 0, 0))
    return pl.pallas_call(
        functools.partial(_hgrn_step_kernel, nb),
        grid=(A_HEADS,),
        in_specs=[colspec, colspec, colspec, vspec, sspec, pl.BlockSpec((1, A_DV), lambda h: (0, 0))],
        out_specs=[vspec, sspec],
        out_shape=[jax.ShapeDtypeStruct(v.shape, F32), jax.ShapeDtypeStruct(s.shape, F32)],
        compiler_params=_cparams(("parallel",)),
        name="hgrn_step",
    )(tr(q), tr(k), tr(lf), v, s, g_norm.reshape(1, A_DV))


def _out_proj_kernel(a_ref, sg_ref, x_ref, w_ref, y_ref):
    y_ref[...] = x_ref[...] + _dot((a_ref[...] * sg_ref[...]).astype(BF16), w_ref[...])


def _out_proj(a, sg, x, w_bf, tm):
    m = x.shape[0]
    row = pl.BlockSpec((tm, D_MODEL), lambda i: (i, 0))
    return pl.pallas_call(
        _out_proj_kernel,
        grid=(m // tm,),
        in_specs=[row, row, row, pl.BlockSpec(w_bf.shape, lambda i: (0, 0))],
        out_specs=row,
        out_shape=jax.ShapeDtypeStruct(x.shape, F32),
        compiler_params=_cparams(("parallel",)),
        name="out_proj",
    )(a, sg, x, w_bf)


def _shared_kv_kernel(x_ref, g_ref, wkv_ref, kn_ref, wf_ref, bf_ref, ind_ref, indt_ref,
                      k_ref, v_ref, lf_ref, chi_ref, cmid_ref, clo_ref, carry_ref):
    tm = x_ref.shape[0]
    u = _rms_rows(x_ref[...], g_ref[...]).astype(BF16)
    bw = B_HEADS * B_DH
    k_ref[...] = _head_rms(_dot(u, wkv_ref[:, 0:bw]), ind_ref[...], indt_ref[...], kn_ref[...])
    v_ref[...] = _dot(u, wkv_ref[:, bw:])
    z = _dot(u, wf_ref[...]) + bf_ref[...]
    lf = jnp.minimum(z, 0.0) - jnp.log1p(jnp.exp(-jnp.abs(z)))
    lf_ref[...] = lf

    @pl.when(pl.program_id(1) == 0)
    def _():
        carry_ref[...] = jnp.zeros_like(carry_ref)

    lower = (lax.broadcasted_iota(jnp.int32, (tm, tm), 0)
             >= lax.broadcasted_iota(jnp.int32, (tm, tm), 1)).astype(BF16)
    c = _sel_dot3(lower, lf) + carry_ref[...]
    carry_ref[...] = c[tm - 1:tm, :]
    hi, mid, lo = _split3(c)
    chi_ref[...] = hi.astype(F32)
    cmid_ref[...] = mid.astype(F32)
    clo_ref[...] = lo.astype(F32)


def _head_indicator():
    ind = (jnp.arange(B_HEADS * B_DH)[:, None] // B_DH == jnp.arange(B_HEADS)[None, :])
    return ind.astype(BF16), ind.T.astype(BF16)


def _shared_kv(x, g, wkv_bf, k_norm, wf_bf, b_fgate, bsz, seq, tm):
    m = x.shape[0]
    n_t = seq // tm
    ind, ind_t = _head_indicator()
    row = pl.BlockSpec((tm, D_MODEL), lambda b, t: (b * n_t + t, 0))
    hrow = pl.BlockSpec((tm, B_HEADS), lambda b, t: (b * n_t + t, 0))
    full = lambda a: pl.BlockSpec(a.shape, lambda b, t: (0,) * a.ndim)
    consts = [g.reshape(1, D_MODEL), wkv_bf, jnp.tile(k_norm, B_HEADS).reshape(1, D_MODEL),
              wf_bf, b_fgate.reshape(1, B_HEADS), ind, ind_t]
    big = jax.ShapeDtypeStruct((m, D_MODEL), F32)
    small = jax.ShapeDtypeStruct((m, B_HEADS), F32)
    return pl.pallas_call(
        _shared_kv_kernel,
        grid=(bsz, n_t),
        in_specs=[row] + [full(a) for a in consts],
        out_specs=[row, row, hrow, hrow, hrow, hrow],
        out_shape=[big, big, small, small, small, small],
        scratch_shapes=[pltpu.VMEM((1, B_HEADS), F32)],
        compiler_params=_cparams(("parallel", "arbitrary")),
        name="shared_kv",
    )(x, *consts)


def _fox_proj_kernel(x_ref, g_ref, w_ref, qn_ref, ind_ref, indt_ref, q_ref, sg_ref):
    xn = _rms_rows(x_ref[...], g_ref[...]).astype(BF16)
    bw = B_HEADS * B_DH
    q = _head_rms(_dot(xn, w_ref[:, 0:bw]), ind_ref[...], indt_ref[...], qn_ref[...])
    q_ref[...] = q * ATTN_SCALE
    pg = _dot(xn, w_ref[:, bw:])
    sg_ref[...] = pg * _sigmoid(pg)


def _fox_proj(x, g, w_bf, q_norm, tm):
    m = x.shape[0]
    ind, ind_t = _head_indicator()
    row = pl.BlockSpec((tm, D_MODEL), lambda i: (i, 0))
    full = lambda a: pl.BlockSpec(a.shape, lambda i: (0,) * a.ndim)
    consts = [g.reshape(1, D_MODEL), w_bf, jnp.tile(q_norm, B_HEADS).reshape(1, D_MODEL), ind, ind_t]
    out = jax.ShapeDtypeStruct((m, D_MODEL), F32)
    return pl.pallas_call(
        _fox_proj_kernel,
        grid=(m // tm,),
        in_specs=[row] + [full(a) for a in consts],
        out_specs=[row, row],
        out_shape=[out, out],
        compiler_params=_cparams(("parallel",)),
        name="fox_proj",
    )(x, *consts)


def _flash_kernel(tq, q_ref, k_ref, v_ref, o_ref):
    qi = pl.program_id(1)
    q = q_ref[0]

    def step(ki, carry, diagonal):
        m, acc = carry
        sl = pl.ds(pl.multiple_of(ki * tq, tq), tq)
        s = lax.dot_general(q, k_ref[0, sl, :], _NT, preferred_element_type=F32)
        if diagonal:
            keep = (lax.broadcasted_iota(jnp.int32, (tq, tq), 0)
                    >= lax.broadcasted_iota(jnp.int32, (tq, tq), 1))
            s = jnp.where(keep, s, -jnp.inf)
        m_new = jnp.maximum(m, jnp.max(s, axis=-1, keepdims=True))
        p = jnp.exp(s - m_new).astype(BF16)
        acc = acc * jnp.exp(m - m_new) + _dot(p, v_ref[0, sl, :])
        return m_new, acc

    init = (jnp.full((tq, 1), -jnp.inf, F32), jnp.zeros((tq, LANES), F32))
    carry = lax.fori_loop(0, qi, lambda ki, c: step(ki, c, False), init)
    _, acc = step(qi, carry, True)
    o_ref[0] = acc / acc[:, B_DH:B_DH + 1]


def _flash(q_aug, k_aug, v_aug, tq):
    g, seq, _ = q_aug.shape
    qspec = pl.BlockSpec((1, tq, LANES), lambda i, j: (i, j, 0))
    kvspec = pl.BlockSpec((1, seq, LANES), lambda i, j: (i, 0, 0))
    return pl.pallas_call(
        functools.partial(_flash_kernel, tq),
        grid=(g, seq // tq),
        in_specs=[qspec, kvspec, kvspec],
        out_specs=qspec,
        out_shape=jax.ShapeDtypeStruct(q_aug.shape, F32),
        compiler_params=_cparams(("parallel", "arbitrary")),
        name="fox_flash",
    )(q_aug, k_aug, v_aug)


def _to_heads(a, bsz, seq):
    return a.reshape(bsz, seq, B_HEADS, -1).transpose(0, 2, 1, 3)


def _fox_attention_prompt(qs, k, v, c, bsz, seq, tq):
    cp = jnp.stack([piece.astype(BF16) for piece in c], axis=-1)
    cp = cp.reshape(bsz, seq, B_HEADS, 3).transpose(0, 2, 1, 3)
    ones = jnp.ones((bsz, B_HEADS, seq, 3), BF16)
    pad = jnp.zeros((bsz, B_HEADS, seq, LANES - B_DH - 6), BF16)
    q_aug = jnp.concatenate([_to_heads(qs, bsz, seq).astype(BF16), cp, ones, pad], axis=-1)
    k_aug = jnp.concatenate([_to_heads(k, bsz, seq).astype(BF16), ones, -cp, pad], axis=-1)
    v_aug = jnp.concatenate([_to_heads(v, bsz, seq).astype(BF16), ones[..., :1],
                             jnp.zeros((bsz, B_HEADS, seq, LANES - B_DH - 1), BF16)], axis=-1)
    g = bsz * B_HEADS
    o = _flash(q_aug.reshape(g, seq, LANES), k_aug.reshape(g, seq, LANES),
               v_aug.reshape(g, seq, LANES), tq)
    o = o.reshape(bsz, B_HEADS, seq, LANES)[..., :B_DH]
    return o.transpose(0, 2, 1, 3).reshape(bsz * seq, B_HEADS * B_DH)


def _decode_kernel(npp, page, pt_ref, qs_ref, qbd_ref, kn_ref, vn_ref, lfn_ref, ind_ref, indt_ref, *rest):
    k_refs = rest[0:npp]
    v_refs = rest[npp:2 * npp]
    lf_refs = rest[2 * npp:3 * npp]
    o_ref = rest[3 * npp]
    m_ref, l_ref, acc_ref, carry_ref = rest[3 * npp + 1:]
    j = pl.program_id(1)
    ind = ind_ref[...]
    ind_t = indt_ref[...]

    @pl.when(j == 0)
    def _():
        prod = qs_ref[0].astype(BF16).astype(F32) * kn_ref[0].astype(BF16).astype(F32)
        m_ref[...] = _dot3(prod, ind)
        l_ref[...] = jnp.ones_like(l_ref)
        acc_ref[...] = vn_ref[0]
        carry_ref[...] = lfn_ref[0]

    upper = (lax.broadcasted_iota(jnp.int32, (page, page), 1)
             > lax.broadcasted_iota(jnp.int32, (page, page), 0)).astype(BF16)
    qbd = qbd_ref[0]
    carry = carry_ref[...]
    logits = []
    for i in range(npp):
        lf = lf_refs[i][0]
        sfx = _sel_dot3(upper, lf)
        s = _dot(k_refs[i][0].astype(BF16), qbd)
        logits.append(s + sfx + carry)
        carry = carry + sfx[0:1, :] + lf[0:1, :]
    carry_ref[...] = carry

    m_old = m_ref[...]
    m_new = m_old
    for lg in logits:
        m_new = jnp.maximum(m_new, jnp.max(lg, axis=0, keepdims=True))
    l_add = jnp.zeros_like(m_new)
    acc_add = jnp.zeros((1, B_HEADS * B_DH), F32)
    for i in range(npp):
        p = jnp.exp(logits[i] - m_new)
        l_add = l_add + jnp.sum(p, axis=0, keepdims=True)
        pexp = _dot(p.astype(BF16), ind_t)
        acc_add = acc_add + jnp.sum(pexp * v_refs[i][0], axis=0, keepdims=True)
    alpha = jnp.exp(m_old - m_new)
    m_ref[...] = m_new
    l_ref[...] = l_ref[...] * alpha + l_add
    acc_ref[...] = acc_ref[...] * _dot3(alpha, ind_t) + acc_add

    @pl.when(j == pl.num_programs(1) - 1)
    def _():
        o_ref[0] = acc_ref[...] / _dot3(l_ref[...], ind_t)


def _fox_attention_sample(qs, k_new, v_new, lf_new, cache_k, cache_v, cache_logf, page_table, npp):
    nb, n_pages = page_table.shape
    n_pool, page = cache_k.shape[:2]
    bw = B_HEADS * B_DH
    ind, ind_t = _head_indicator()
    qbd = (qs[:, :, None] * ind.astype(F32)[None]).astype(BF16)
    ck = cache_k.reshape(n_pool, page, bw)
    cv = cache_v.reshape(n_pool, page, bw)
    n_steps = n_pages // npp

    def page_map(i):
        def index_map(b, j, pt):
            return (pt[b * n_pages + (n_pages - 1 - (j * npp + i))], 0, 0)
        return index_map

    per_seq = lambda width: pl.BlockSpec((1, 1, width), lambda b, j, pt: (b, 0, 0))
    full = lambda a: pl.BlockSpec(a.shape, lambda b, j, pt: (0,) * a.ndim)
    in_specs = [per_seq(bw), pl.BlockSpec((1, bw, B_HEADS), lambda b, j, pt: (b, 0, 0)),
                per_seq(bw), per_seq(bw), per_seq(B_HEADS), full(ind), full(ind_t)]
    in_specs += [pl.BlockSpec((1, page, bw), page_map(i)) for i in range(npp)]
    in_specs += [pl.BlockSpec((1, page, bw), page_map(i)) for i in range(npp)]
    in_specs += [pl.BlockSpec((1, page, B_HEADS), page_map(i)) for i in range(npp)]
    grid_spec = pltpu.PrefetchScalarGridSpec(
        num_scalar_prefetch=1,
        grid=(nb, n_steps),
        in_specs=in_specs,
        out_specs=per_seq(bw),
        scratch_shapes=[pltpu.VMEM((1, B_HEADS), F32), pltpu.VMEM((1, B_HEADS), F32),
                        pltpu.VMEM((1, bw), F32), pltpu.VMEM((1, B_HEADS), F32)],
    )
    r3 = lambda a: a.reshape(nb, 1, -1)
    o = pl.pallas_call(
        functools.partial(_decode_kernel, npp, page),
        grid_spec=grid_spec,
        out_shape=jax.ShapeDtypeStruct((nb, 1, bw), F32),
        compiler_params=_cparams(("parallel", "arbitrary")),
        name="fox_decode",
    )(page_table.reshape(-1), r3(qs), qbd, r3(k_new), r3(v_new), r3(lf_new), ind, ind_t,
      *([ck] * npp), *([cv] * npp), *([cache_logf] * npp))
    return o.reshape(nb, bw)


def _pick(n, pref):
    return pref if n % pref == 0 else n


def kernel(x_prompt, x_sample, state_hgrn, cache_k, cache_v, cache_logf, page_table,
           norm_in, w_in_a, lb_logits, g_norm_a, w_o_a, norm_kv, w_kv, k_norm, w_fgate, b_fgate,
           w_in_b, q_norm_b, w_o_b):
    bsz, seq, _ = x_prompt.shape
    nb = x_sample.shape[0]
    n_a = w_in_a.shape[0]
    n_b = w_in_b.shape[0]
    tm = _pick(seq, 256)
    tb = _pick(seq, 512)
    tq = _pick(seq, 256)
    npp = _pick(page_table.shape[1], 8)

    hp = x_prompt.reshape(bsz * seq, D_MODEL)
    hs = x_sample.reshape(nb, D_MODEL)
    zero_state = jnp.zeros((bsz, A_HEADS, A_DK, A_DV), F32)
    sp, ss = [], []
    for l in range(n_a):
        w_in = w_in_a[l].astype(BF16)
        w_o = w_o_a[l].astype(BF16)
        q, k, lf, v, sg = _hgrn_proj(hp, norm_in[l], w_in, lb_logits, l, tm)
        o, s_new = _hgrn_rec(q, k, lf, v, zero_state, g_norm_a[l], bsz, seq, tb)
        hp = _out_proj(o, sg, hp, w_o, tm)
        sp.append(s_new)
        q, k, lf, v, sg = _hgrn_proj(hs, norm_in[l], w_in, lb_logits, l, nb)
        o, s_new = _hgrn_step(q, k, lf, v, state_hgrn[l], g_norm_a[l])
        hs = _out_proj(o, sg, hs, w_o, nb)
        ss.append(s_new)

    wkv = w_kv.astype(BF16)
    wf = w_fgate.astype(BF16)
    kp, vp, lfp, *cp = _shared_kv(hp, norm_kv, wkv, k_norm, wf, b_fgate, bsz, seq, tm)
    ks, vs, lfs, *_ = _shared_kv(hs, norm_kv, wkv, k_norm, wf, b_fgate, 1, nb, nb)

    for j in range(n_b):
        w_in = w_in_b[j].astype(BF16)
        w_o = w_o_b[j].astype(BF16)
        g = norm_in[n_a + j]
        qs, sg = _fox_proj(hp, g, w_in, q_norm_b[j], tm)
        o = _fox_attention_prompt(qs, kp, vp, cp, bsz, seq, tq)
        hp = _out_proj(o, sg, hp, w_o, tm)
        qs, sg = _fox_proj(hs, g, w_in, q_norm_b[j], nb)
        o = _fox_attention_sample(qs, ks, vs, lfs, cache_k, cache_v, cache_logf, page_table, npp)
        hs = _out_proj(o, sg, hs, w_o, nb)

    return (hp.reshape(bsz, seq, D_MODEL), hs.reshape(nb, 1, D_MODEL),
            jnp.stack(sp), jnp.stack(ss),
            kp.reshape(bsz, seq, B_HEADS, B_DH), ks.reshape(nb, 1, B_HEADS, B_DH),
            vp.reshape(bsz, seq, B_HEADS, B_DH), vs.reshape(nb, 1, B_HEADS, B_DH),
            lfp.reshape(bsz, seq, B_HEADS), lfs.reshape(nb, 1, B_HEADS))
```

```python
import functools

import math

import jax
import jax.numpy as jnp
import numpy as np
from jax import lax
from jax.experimental import pallas as pl
from jax.experimental.pallas import tpu as pltpu

F32 = jnp.float32
BF16 = jnp.bfloat16

D_MODEL = 1024
A_HEADS = 8
A_DK = 128
A_DV = 128
B_HEADS = 16
B_DH = 64
RMS_EPS = 1e-6
ATTN_SCALE = B_DH ** -0.5
LOG2E = math.log2(math.e)

LANES = 128
SUBLANES = 8
REC_HEADS = 4
REC_CHUNK = 128
VMEM_LIMIT = 48 * 1024 * 1024

_NT = (((1,), (1,)), ((), ()))
_TN = (((0,), (0,)), ((), ()))


def _cparams(sem):
    return pltpu.CompilerParams(dimension_semantics=sem, vmem_limit_bytes=VMEM_LIMIT)


def _dot(a, b):
    return jnp.dot(a, b, preferred_element_type=F32)


def _rms_rows(x, g):
    ms = jnp.mean(x * x, axis=-1, keepdims=True)
    return x * lax.rsqrt(ms + RMS_EPS) * g


def _split3(x):
    hi = x.astype(BF16)
    r1 = x - hi.astype(F32)
    mid = r1.astype(BF16)
    lo = (r1 - mid.astype(F32)).astype(BF16)
    return hi, mid, lo


def _dot3(x, sel):
    hi, mid, lo = _split3(x)
    return _dot(hi, sel) + _dot(mid, sel) + _dot(lo, sel)


def _sel_dot3(sel, x):
    hi, mid, lo = _split3(x)
    return _dot(sel, hi) + _dot(sel, mid) + _dot(sel, lo)


def _sigmoid(x):
    return 1.0 / (1.0 + jnp.exp(-x))


def _head_rms(x, ind, ind_t, gain):
    ssum = _dot3(x * x, ind)
    r = lax.rsqrt(ssum * (1.0 / B_DH) + RMS_EPS)
    return x * _dot3(r, ind_t) * gain


def _hgrn_proj_kernel(layer, x_ref, g_ref, w_ref, lbl_ref, q_ref, k_ref, lf_ref, v_ref, sg_ref):
    xn = _rms_rows(x_ref[...], g_ref[...]).astype(BF16)
    lbl = lbl_ref[...]
    e = jnp.exp(lbl - jnp.max(lbl, axis=0, keepdims=True))
    p = e / jnp.sum(e, axis=0, keepdims=True)
    csum = p[0:1, :]
    for i in range(1, layer + 1):
        csum = csum + p[i:i + 1, :]
    lb = csum - p[0:1, :]
    log_lb = jnp.log(lb)
    log_1m_lb = jnp.log1p(-lb)
    hk = A_HEADS * A_DK

    pq = _dot(xn, w_ref[:, 0:hk])
    q_ref[...] = pq * _sigmoid(pq)

    fz = _dot(xn, w_ref[:, hk:2 * hk])
    ez = jnp.exp(-jnp.abs(fz))
    log_sig = jnp.minimum(fz, 0.0) - jnp.log1p(ez)
    sig_neg = jnp.where(fz >= 0.0, ez, 1.0) / (1.0 + ez)
    b = log_1m_lb + log_sig
    mx = jnp.maximum(log_lb, b)
    lf_ref[...] = mx + jnp.log1p(jnp.exp(-jnp.abs(log_lb - b)))
    k_ref[...] = (1.0 - lb) * sig_neg

    v_ref[...] = _dot(xn, w_ref[:, 2 * hk:2 * hk + A_HEADS * A_DV])
    pg = _dot(xn, w_ref[:, 2 * hk + A_HEADS * A_DV:])
    sg_ref[...] = pg * _sigmoid(pg)


def _hgrn_proj(x, g, w_bf, lb_logits, layer, tm):
    m = x.shape[0]
    n_in = w_bf.shape[1]
    row = pl.BlockSpec((tm, D_MODEL), lambda i: (i, 0))
    full = lambda shape: pl.BlockSpec(shape, lambda i: (0, 0))
    out = jax.ShapeDtypeStruct((m, D_MODEL), F32)
    return pl.pallas_call(
        functools.partial(_hgrn_proj_kernel, layer),
        grid=(m // tm,),
        in_specs=[row, full((1, D_MODEL)), full((D_MODEL, n_in)), full(lb_logits.shape)],
        out_specs=[row] * 5,
        out_shape=[out] * 5,
        compiler_params=_cparams(("parallel",)),
        name="hgrn_proj",
    )(x, g.reshape(1, D_MODEL), w_bf, lb_logits)


def _rec_chunk(q, k, v, lf, s_t):
    c = q.shape[0]
    nt = c // SUBLANES
    tiles = lambda x: [x[SUBLANES * i:SUBLANES * (i + 1), :] for i in range(nt)]
    cat = lambda ts: jnp.concatenate(ts, axis=0).astype(BF16)
    nt_dot = lambda a, b: lax.dot_general(a, b, _NT, preferred_element_type=F32)
    xor = lax.broadcasted_iota(jnp.int32, (c, c), 0) ^ lax.broadcasted_iota(jnp.int32, (c, c), 1)
    att = jnp.where(xor == 0, nt_dot(q.astype(BF16), k.astype(BF16)), 0.0)
    q_t, k_t = tiles(q), tiles(k)
    pre = tiles(lf)
    tot = tiles(lf)

    row = lax.broadcasted_iota(jnp.int32, (SUBLANES, LANES), 0)
    m = 1
    while m < SUBLANES:
        right = (row & m) != 0
        qt, kt = [], []
        for i in range(nt):
            e = jnp.exp(jnp.where(right, pre[i], tot[i] - pre[i]))
            qt.append(jnp.where(right, q_t[i] * e, 0.0))
            kt.append(jnp.where(right, 0.0, k_t[i] * e))
            down = pltpu.roll(tot[i], m, 0)
            up = pltpu.roll(tot[i], SUBLANES - m, 0)
            pre[i] = pre[i] + jnp.where(right, down, 0.0)
            tot[i] = tot[i] + jnp.where(right, down, up)
        att = att + jnp.where(xor < 2 * m, nt_dot(cat(qt), cat(kt)), 0.0)
        m *= 2

    zero = jnp.zeros((SUBLANES, LANES), F32)
    mv = 1
    while mv < nt:
        qt, kt = [zero] * nt, [zero] * nt
        for base in range(0, nt, 2 * mv):
            for j in range(mv):
                lt, rt = base + j, base + mv + j
                kt[lt] = k_t[lt] * jnp.exp(tot[lt] - pre[lt])
                qt[rt] = q_t[rt] * jnp.exp(pre[rt])
        a = nt_dot(cat(qt), cat(kt))
        att = att + (a if 2 * mv == nt else jnp.where(xor < 2 * mv * SUBLANES, a, 0.0))
        for base in range(0, nt, 2 * mv):
            left_total = tot[base]
            both = left_total + tot[base + mv]
            for j in range(mv):
                pre[base + mv + j] = pre[base + mv + j] + left_total
            for j in range(2 * mv):
                tot[base + j] = both
        mv *= 2

    vb = v.astype(BF16)
    qg = cat([q_t[i] * jnp.exp(pre[i]) for i in range(nt)])
    kh = cat([k_t[i] * jnp.exp(tot[i] - pre[i]) for i in range(nt)])
    o = nt_dot(qg, s_t.astype(BF16)) + _dot(att.astype(BF16), vb)
    s_new = s_t * jnp.exp(tot[0][0:1, :]) + lax.dot_general(vb, kh, _TN, preferred_element_type=F32)
    return o, s_new


def _hgrn_rec_kernel(n_chunks, nh, q_ref, k_ref, lf_ref, v_ref, s0_ref, gn_ref, o_ref, s_out_ref, st_ref):
    t = pl.program_id(2)

    @pl.when(t == 0)
    def _():
        for h in range(nh):
            st_ref[h] = s0_ref[0, h].T

    def body(ci, carry):
        rows = pl.ds(pl.multiple_of(ci * REC_CHUNK, REC_CHUNK), REC_CHUNK)
        for h in range(nh):
            cols = slice(h * A_DK, (h + 1) * A_DK)
            o, s_new = _rec_chunk(q_ref[rows, cols], k_ref[rows, cols], v_ref[rows, cols],
                                  lf_ref[rows, cols], st_ref[h])
            st_ref[h] = s_new
            o_ref[rows, cols] = _rms_rows(o, gn_ref[...])
        return carry

    lax.fori_loop(0, n_chunks, body, 0)

    @pl.when(t == pl.num_programs(2) - 1)
    def _():
        for h in range(nh):
            s_out_ref[0, h] = st_ref[h].T


def _hgrn_rec(q, k, lf, v, s0, g_norm, bsz, seq, tb, nh):
    n_t = seq // tb
    blk = pl.BlockSpec((tb, nh * A_DK), lambda b, h, t: (b * n_t + t, h))
    st = pl.BlockSpec((1, nh, A_DK, A_DV), lambda b, h, t: (b, h, 0, 0))
    return pl.pallas_call(
        functools.partial(_hgrn_rec_kernel, tb // REC_CHUNK, nh),
        grid=(bsz, A_HEADS // nh, n_t),
        in_specs=[blk, blk, blk, blk, st, pl.BlockSpec((1, A_DV), lambda b, h, t: (0, 0))],
        out_specs=[blk, st],
        out_shape=[jax.ShapeDtypeStruct(q.shape, F32), jax.ShapeDtypeStruct(s0.shape, F32)],
        scratch_shapes=[pltpu.VMEM((nh, A_DV, A_DK), F32)],
        compiler_params=_cparams(("parallel", "parallel", "arbitrary")),
        name="hgrn_rec",
    )(q, k, lf, v, s0, g_norm.reshape(1, A_DV))


def _hgrn_step_kernel(nb, qt_ref, kt_ref, lft_ref, v_ref, s_ref, gn_ref, o_ref, s_out_ref):
    qt = qt_ref[0]
    kt = kt_ref[0]
    ft = jnp.exp(lft_ref[0])
    for b in range(nb):
        col = lambda a: jnp.broadcast_to(a[:, b:b + 1], (A_DK, A_DV))
        s_new = col(ft) * s_ref[b, 0] + col(kt) * v_ref[b:b + 1, :]
        s_out_ref[b, 0] = s_new
        o = jnp.sum(col(qt) * s_new, axis=0, keepdims=True)
        o_ref[b:b + 1, :] = _rms_rows(o, gn_ref[...])


def _hgrn_step(q, k, lf, v, s, g_norm):
    nb = q.shape[0]
    tr = lambda a: a.reshape(nb, A_HEADS, A_DK).transpose(1, 2, 0)
    colspec = pl.BlockSpec((1, A_DK, nb), lambda h: (h, 0, 0))
    vspec = pl.BlockSpec((nb, A_DV), lambda h: (0, h))
    sspec = pl.BlockSpec((nb, 1, A_DK, A_DV), lambda h: (0, h, 0, 0))
    return pl.pallas_call(
        functools.partial(_hgrn_step_kernel, nb),
        grid=(A_HEADS,),
        in_specs=[colspec, colspec, colspec, vspec, sspec, pl.BlockSpec((1, A_DV), lambda h: (0, 0))],
        out_specs=[vspec, sspec],
        out_shape=[jax.ShapeDtypeStruct(v.shape, F32), jax.ShapeDtypeStruct(s.shape, F32)],
        compiler_params=_cparams(("parallel",)),
        name="hgrn_step",
    )(tr(q), tr(k), tr(lf), v, s, g_norm.reshape(1, A_DV))


def _out_proj_kernel(a_ref, sg_ref, x_ref, w_ref, y_ref):
    y_ref[...] = x_ref[...] + _dot((a_ref[...] * sg_ref[...]).astype(BF16), w_ref[...])


def _out_proj(a, sg, x, w_bf, tm):
    m = x.shape[0]
    row = pl.BlockSpec((tm, D_MODEL), lambda i: (i, 0))
    return pl.pallas_call(
        _out_proj_kernel,
        grid=(m // tm,),
        in_specs=[row, row, row, pl.BlockSpec(w_bf.shape, lambda i: (0, 0))],
        out_specs=row,
        out_shape=jax.ShapeDtypeStruct(x.shape, F32),
        compiler_params=_cparams(("parallel",)),
        name="out_proj",
    )(a, sg, x, w_bf)


def _aug_constants():
    width = B_HEADS * LANES
    place_q = np.zeros((3, B_HEADS, width), np.float32)
    place_k = np.zeros((3, B_HEADS, width), np.float32)
    ones_q = np.zeros((1, width), np.float32)
    ones_k = np.zeros((1, width), np.float32)
    one_v = np.zeros((1, width), np.float32)
    for h in range(B_HEADS):
        base = h * LANES + (B_DH if h % 2 == 0 else 0)
        one_v[0, base] = 1.0
        for p in range(3):
            place_q[p, h, base + p] = 1.0
            ones_q[0, base + 3 + p] = 1.0
            ones_k[0, base + p] = 1.0
            place_k[p, h, base + 3 + p] = -1.0
    as_bf = lambda a: jnp.asarray(a, BF16)
    return as_bf(place_q), jnp.asarray(ones_q), as_bf(place_k), jnp.asarray(ones_k), jnp.asarray(one_v)


def _bias_columns(pieces, place_ref, ones_ref):
    out = ones_ref[...]
    for p in range(3):
        out = out + _dot(pieces[p], place_ref[p])
    return out


def _store_heads(x, other, o_ref):
    lane = lax.broadcasted_iota(jnp.int32, (x.shape[0], LANES), 1)
    for h in range(B_HEADS):
        own = (lane < B_DH) if h % 2 == 0 else (lane >= B_DH)
        xv = x[:, (h // 2) * LANES:(h // 2 + 1) * LANES]
        o_ref[0, h] = jnp.where(own, xv, other[:, h * LANES:(h + 1) * LANES]).astype(BF16)


def _shared_kv_kernel(x_ref, g_ref, wkv_ref, kn_ref, wf_ref, bf_ref, ind_ref, indt_ref,
                      place_ref, ones_ref, onev_ref,
                      k_ref, v_ref, lf_ref, chi_ref, cmid_ref, clo_ref, kaug_ref, vaug_ref, carry_ref):
    tm = x_ref.shape[0]
    u = _rms_rows(x_ref[...], g_ref[...]).astype(BF16)
    bw = B_HEADS * B_DH
    k = _head_rms(_dot(u, wkv_ref[:, 0:bw]), ind_ref[...], indt_ref[...], kn_ref[...])
    k_ref[...] = k
    v = _dot(u, wkv_ref[:, bw:])
    v_ref[...] = v
    z = _dot(u, wf_ref[...]) + bf_ref[...]
    lf = jnp.minimum(z, 0.0) - jnp.log1p(jnp.exp(-jnp.abs(z)))
    lf_ref[...] = lf

    @pl.when(pl.program_id(1) == 0)
    def _():
        carry_ref[...] = jnp.zeros_like(carry_ref)

    lower = (lax.broadcasted_iota(jnp.int32, (tm, tm), 0)
             >= lax.broadcasted_iota(jnp.int32, (tm, tm), 1)).astype(BF16)
    c = _sel_dot3(lower, lf) + carry_ref[...]
    carry_ref[...] = c[tm - 1:tm, :]
    pieces = _split3(c * LOG2E)
    chi_ref[...], cmid_ref[...], clo_ref[...] = pieces
    _store_heads(k, _bias_columns(pieces, place_ref, ones_ref), kaug_ref)
    _store_heads(v, onev_ref[...], vaug_ref)


def _head_indicator():
    ind = (jnp.arange(B_HEADS * B_DH)[:, None] // B_DH == jnp.arange(B_HEADS)[None, :])
    return ind.astype(BF16), ind.T.astype(BF16)


def _shared_kv(x, g, wkv_bf, k_norm, wf_bf, b_fgate, bsz, seq, tm):
    m = x.shape[0]
    n_t = seq // tm
    ind, ind_t = _head_indicator()
    _, _, place_k, ones_k, one_v = _aug_constants()
    row = pl.BlockSpec((tm, D_MODEL), lambda b, t: (b * n_t + t, 0))
    hrow = pl.BlockSpec((tm, B_HEADS), lambda b, t: (b * n_t + t, 0))
    heads = pl.BlockSpec((1, B_HEADS, tm, LANES), lambda b, t: (b, 0, t, 0))
    full = lambda a: pl.BlockSpec(a.shape, lambda b, t: (0,) * a.ndim)
    consts = [g.reshape(1, D_MODEL), wkv_bf, jnp.tile(k_norm, B_HEADS).reshape(1, D_MODEL),
              wf_bf, b_fgate.reshape(1, B_HEADS), ind, ind_t, place_k, ones_k, one_v]
    big = jax.ShapeDtypeStruct((m, D_MODEL), F32)
    small = jax.ShapeDtypeStruct((m, B_HEADS), F32)
    piece = jax.ShapeDtypeStruct((m, B_HEADS), BF16)
    aug = jax.ShapeDtypeStruct((bsz, B_HEADS, seq, LANES), BF16)
    return pl.pallas_call(
        _shared_kv_kernel,
        grid=(bsz, n_t),
        in_specs=[row] + [full(a) for a in consts],
        out_specs=[row, row, hrow, hrow, hrow, hrow, heads, heads],
        out_shape=[big, big, small, piece, piece, piece, aug, aug],
        scratch_shapes=[pltpu.VMEM((1, B_HEADS), F32)],
        compiler_params=_cparams(("parallel", "arbitrary")),
        name="shared_kv",
    )(x, *consts)


def _fox_qg(x_ref, g_ref, w_ref, qn_ref, ind_ref, indt_ref):
    xn = _rms_rows(x_ref[...], g_ref[...]).astype(BF16)
    bw = B_HEADS * B_DH
    q = _head_rms(_dot(xn, w_ref[:, 0:bw]), ind_ref[...], indt_ref[...], qn_ref[...])
    pg = _dot(xn, w_ref[:, bw:])
    return q, pg * _sigmoid(pg)


def _fox_proj_kernel(x_ref, g_ref, w_ref, qn_ref, ind_ref, indt_ref, q_ref, sg_ref):
    q, sg = _fox_qg(x_ref, g_ref, w_ref, qn_ref, ind_ref, indt_ref)
    q_ref[...] = q * ATTN_SCALE
    sg_ref[...] = sg


def _fox_proj_aug_kernel(x_ref, g_ref, w_ref, qn_ref, ind_ref, indt_ref, chi_ref, cmid_ref, clo_ref,
                         place_ref, ones_ref, qaug_ref, sg_ref):
    q, sg = _fox_qg(x_ref, g_ref, w_ref, qn_ref, ind_ref, indt_ref)
    sg_ref[...] = sg
    pieces = (chi_ref[...], cmid_ref[...], clo_ref[...])
    _store_heads(q * (ATTN_SCALE * LOG2E), _bias_columns(pieces, place_ref, ones_ref), qaug_ref)


def _fox_proj(x, g, w_bf, q_norm, tm):
    m = x.shape[0]
    ind, ind_t = _head_indicator()
    row = pl.BlockSpec((tm, D_MODEL), lambda i: (i, 0))
    full = lambda a: pl.BlockSpec(a.shape, lambda i: (0,) * a.ndim)
    consts = [g.reshape(1, D_MODEL), w_bf, jnp.tile(q_norm, B_HEADS).reshape(1, D_MODEL), ind, ind_t]
    out = jax.ShapeDtypeStruct((m, D_MODEL), F32)
    return pl.pallas_call(
        _fox_proj_kernel,
        grid=(m // tm,),
        in_specs=[row] + [full(a) for a in consts],
        out_specs=[row, row],
        out_shape=[out, out],
        compiler_params=_cparams(("parallel",)),
        name="fox_proj",
    )(x, *consts)


def _fox_proj_aug(x, g, w_bf, q_norm, c_pieces, bsz, seq, tm):
    m = x.shape[0]
    n_t = seq // tm
    ind, ind_t = _head_indicator()
    place_q, ones_q, _, _, _ = _aug_constants()
    row = pl.BlockSpec((tm, D_MODEL), lambda b, t: (b * n_t + t, 0))
    hrow = pl.BlockSpec((tm, B_HEADS), lambda b, t: (b * n_t + t, 0))
    heads = pl.BlockSpec((1, B_HEADS, tm, LANES), lambda b, t: (b, 0, t, 0))
    full = lambda a: pl.BlockSpec(a.shape, lambda b, t: (0,) * a.ndim)
    consts = [g.reshape(1, D_MODEL), w_bf, jnp.tile(q_norm, B_HEADS).reshape(1, D_MODEL), ind, ind_t]
    return pl.pallas_call(
        _fox_proj_aug_kernel,
        grid=(bsz, n_t),
        in_specs=[row] + [full(a) for a in consts] + [hrow] * 3 + [full(place_q), full(ones_q)],
        out_specs=[heads, row],
        out_shape=[jax.ShapeDtypeStruct((bsz, B_HEADS, seq, LANES), BF16),
                   jax.ShapeDtypeStruct((m, D_MODEL), F32)],
        compiler_params=_cparams(("parallel", "parallel")),
        name="fox_proj_aug",
    )(x, *consts, *c_pieces, place_q, ones_q)


def _flash_kernel(tq, q_ref, k_ref, v_ref, o_ref, s_ref, m_ref, acc_ref):
    qi = pl.program_id(2)

    def block(ki):
        return pl.ds(pl.multiple_of(ki * tq, tq), tq)

    def put_scores(slot, ki):
        for h in range(2):
            s_ref[slot, h] = lax.dot_general(q_ref[0, h], k_ref[0, h, block(ki), :], _NT,
                                             preferred_element_type=F32)

    def consume(slot, ki, diagonal):
        for h in range(2):
            sh = s_ref[slot, h]
            if diagonal:
                keep = (lax.broadcasted_iota(jnp.int32, (tq, tq), 0)
                        >= lax.broadcasted_iota(jnp.int32, (tq, tq), 1))
                sh = jnp.where(keep, sh, -jnp.inf)
            m = m_ref[h]
            m_new = jnp.maximum(m, jnp.max(sh, axis=-1, keepdims=True))
            p = jnp.exp2(sh - m_new).astype(BF16)
            acc_ref[h] = acc_ref[h] * jnp.exp2(m - m_new) + _dot(p, v_ref[0, h, block(ki), :])
            m_ref[h] = m_new

    m_ref[...] = jnp.full(m_ref.shape, -jnp.inf, F32)
    acc_ref[...] = jnp.zeros(acc_ref.shape, F32)
    put_scores(0, 0)

    def body(j, carry):
        ki = 2 * j
        put_scores(1, ki + 1)
        consume(0, ki, False)
        put_scores(0, ki + 2)
        consume(1, ki + 1, False)
        return carry

    lax.fori_loop(0, qi // 2, body, 0)

    @pl.when(qi % 2 == 0)
    def _():
        consume(0, qi, True)

    @pl.when(qi % 2 == 1)
    def _():
        put_scores(1, qi)
        consume(0, qi - 1, False)
        consume(1, qi, True)

    lane = lax.broadcasted_iota(jnp.int32, (tq, LANES), 1)
    acc0 = acc_ref[0]
    acc1 = acc_ref[1]
    o_ref[...] = jnp.where(lane < B_DH, acc0 / acc0[:, B_DH:B_DH + 1], acc1 / acc1[:, 0:1])


def _flash(q_aug, k_aug, v_aug, tq):
    bsz, _, seq, _ = q_aug.shape
    n_q = seq // tq
    qspec = pl.BlockSpec((1, 2, tq, LANES), lambda b, hp, i: (b, hp, i, 0))
    kvspec = pl.BlockSpec((1, 2, seq, LANES), lambda b, hp, i: (b, hp, 0, 0))
    return pl.pallas_call(
        functools.partial(_flash_kernel, tq),
        grid=(bsz, B_HEADS // 2, n_q),
        in_specs=[qspec, kvspec, kvspec],
        out_specs=pl.BlockSpec((tq, LANES), lambda b, hp, i: (b * n_q + i, hp)),
        out_shape=jax.ShapeDtypeStruct((bsz * seq, B_HEADS * B_DH), F32),
        scratch_shapes=[pltpu.VMEM((2, 2, tq, tq), F32), pltpu.VMEM((2, tq, 1), F32),
                        pltpu.VMEM((2, tq, LANES), F32)],
        compiler_params=_cparams(("parallel", "parallel", "arbitrary")),
        name="fox_flash",
    )(q_aug, k_aug, v_aug)


def _decode_kernel(npp, page, pt_ref, qs_ref, qbd_ref, kn_ref, vn_ref, lfn_ref, ind_ref, indt_ref, *rest):
    k_refs = rest[0:npp]
    v_refs = rest[npp:2 * npp]
    lf_refs = rest[2 * npp:3 * npp]
    o_ref = rest[3 * npp]
    m_ref, l_ref, acc_ref, carry_ref = rest[3 * npp + 1:]
    j = pl.program_id(1)
    ind = ind_ref[...]
    ind_t = indt_ref[...]

    @pl.when(j == 0)
    def _():
        prod = qs_ref[0].astype(BF16).astype(F32) * kn_ref[0].astype(BF16).astype(F32)
        m_ref[...] = _dot3(prod, ind)
        l_ref[...] = jnp.ones_like(l_ref)
        acc_ref[...] = vn_ref[0]
        carry_ref[...] = lfn_ref[0]

    upper = (lax.broadcasted_iota(jnp.int32, (page, page), 1)
             > lax.broadcasted_iota(jnp.int32, (page, page), 0)).astype(BF16)
    qbd = qbd_ref[0]
    carry = carry_ref[...]
    logits = []
    for i in range(npp):
        lf = lf_refs[i][0]
        sfx = _sel_dot3(upper, lf)
        s = _dot(k_refs[i][0].astype(BF16), qbd)
        logits.append(s + sfx + carry)
        carry = carry + sfx[0:1, :] + lf[0:1, :]
    carry_ref[...] = carry

    m_old = m_ref[...]
    m_new = m_old
    for lg in logits:
        m_new = jnp.maximum(m_new, jnp.max(lg, axis=0, keepdims=True))
    l_add = jnp.zeros_like(m_new)
    acc_add = jnp.zeros((1, B_HEADS * B_DH), F32)
    for i in range(npp):
        p = jnp.exp(logits[i] - m_new)
        l_add = l_add + jnp.sum(p, axis=0, keepdims=True)
        pexp = _dot(p.astype(BF16), ind_t)
        acc_add = acc_add + jnp.sum(pexp * v_refs[i][0], axis=0, keepdims=True)
    alpha = jnp.exp(m_old - m_new)
    m_ref[...] = m_new
    l_ref[...] = l_ref[...] * alpha + l_add
    acc_ref[...] = acc_ref[...] * _dot3(alpha, ind_t) + acc_add

    @pl.when(j == pl.num_programs(1) - 1)
    def _():
        o_ref[0] = acc_ref[...] / _dot3(l_ref[...], ind_t)


def _fox_attention_sample(qs, k_new, v_new, lf_new, cache_k, cache_v, cache_logf, page_table, npp):
    nb, n_pages = page_table.shape
    n_pool, page = cache_k.shape[:2]
    bw = B_HEADS * B_DH
    ind, ind_t = _head_indicator()
    qbd = (qs[:, :, None] * ind.astype(F32)[None]).astype(BF16)
    ck = cache_k.reshape(n_pool, page, bw)
    cv = cache_v.reshape(n_pool, page, bw)
    n_steps = n_pages // npp

    def page_map(i):
        def index_map(b, j, pt):
            return (pt[b * n_pages + (n_pages - 1 - (j * npp + i))], 0, 0)
        return index_map

    per_seq = lambda width: pl.BlockSpec((1, 1, width), lambda b, j, pt: (b, 0, 0))
    full = lambda a: pl.BlockSpec(a.shape, lambda b, j, pt: (0,) * a.ndim)
    in_specs = [per_seq(bw), pl.BlockSpec((1, bw, B_HEADS), lambda b, j, pt: (b, 0, 0)),
                per_seq(bw), per_seq(bw), per_seq(B_HEADS), full(ind), full(ind_t)]
    in_specs += [pl.BlockSpec((1, page, bw), page_map(i)) for i in range(npp)]
    in_specs += [pl.BlockSpec((1, page, bw), page_map(i)) for i in range(npp)]
    in_specs += [pl.BlockSpec((1, page, B_HEADS), page_map(i)) for i in range(npp)]
    grid_spec = pltpu.PrefetchScalarGridSpec(
        num_scalar_prefetch=1,
        grid=(nb, n_steps),
        in_specs=in_specs,
        out_specs=per_seq(bw),
        scratch_shapes=[pltpu.VMEM((1, B_HEADS), F32), pltpu.VMEM((1, B_HEADS), F32),
                        pltpu.VMEM((1, bw), F32), pltpu.VMEM((1, B_HEADS), F32)],
    )
    r3 = lambda a: a.reshape(nb, 1, -1)
    o = pl.pallas_call(
        functools.partial(_decode_kernel, npp, page),
        grid_spec=grid_spec,
        out_shape=jax.ShapeDtypeStruct((nb, 1, bw), F32),
        compiler_params=_cparams(("parallel", "arbitrary")),
        name="fox_decode",
    )(page_table.reshape(-1), r3(qs), qbd, r3(k_new), r3(v_new), r3(lf_new), ind, ind_t,
      *([ck] * npp), *([cv] * npp), *([cache_logf] * npp))
    return o.reshape(nb, bw)


def _pick(n, pref):
    return pref if n % pref == 0 else n


def kernel(x_prompt, x_sample, state_hgrn, cache_k, cache_v, cache_logf, page_table,
           norm_in, w_in_a, lb_logits, g_norm_a, w_o_a, norm_kv, w_kv, k_norm, w_fgate, b_fgate,
           w_in_b, q_norm_b, w_o_b):
    bsz, seq, _ = x_prompt.shape
    nb = x_sample.shape[0]
    n_a = w_in_a.shape[0]
    n_b = w_in_b.shape[0]
    tm = _pick(seq, 256)
    tb = _pick(seq, 512)
    tq = _pick(seq, 512)
    npp = _pick(page_table.shape[1], 8)

    hp = x_prompt.reshape(bsz * seq, D_MODEL)
    hs = x_sample.reshape(nb, D_MODEL)
    zero_state = jnp.zeros((bsz, A_HEADS, A_DK, A_DV), F32)
    sp, ss = [], []
    for l in range(n_a):
        w_in = w_in_a[l].astype(BF16)
        w_o = w_o_a[l].astype(BF16)
        q, k, lf, v, sg = _hgrn_proj(hp, norm_in[l], w_in, lb_logits, l, tm)
        o, s_new = _hgrn_rec(q, k, lf, v, zero_state, g_norm_a[l], bsz, seq, tb, REC_HEADS)
        hp = _out_proj(o, sg, hp, w_o, tm)
        sp.append(s_new)
        q, k, lf, v, sg = _hgrn_proj(hs, norm_in[l], w_in, lb_logits, l, nb)
        o, s_new = _hgrn_step(q, k, lf, v, state_hgrn[l], g_norm_a[l])
        hs = _out_proj(o, sg, hs, w_o, nb)
        ss.append(s_new)

    wkv = w_kv.astype(BF16)
    wf = w_fgate.astype(BF16)
    kp, vp, lfp, *c_pieces, k_aug, v_aug = _shared_kv(hp, norm_kv, wkv, k_norm, wf, b_fgate, bsz, seq, tm)
    ks, vs, lfs, *_ = _shared_kv(hs, norm_kv, wkv, k_norm, wf, b_fgate, 1, nb, nb)

    for j in range(n_b):
        w_in = w_in_b[j].astype(BF16)
        w_o = w_o_b[j].astype(BF16)
        g = norm_in[n_a + j]
        q_aug, sg = _fox_proj_aug(hp, g, w_in, q_norm_b[j], c_pieces, bsz, seq, tm)
        o = _flash(q_aug, k_aug, v_aug, tq)
        hp = _out_proj(o, sg, hp, w_o, tm)
        qs, sg = _fox_proj(hs, g, w_in, q_norm_b[j], nb)
        o = _fox_attention_sample(qs, ks, vs, lfs, cache_k, cache_v, cache_logf, page_table, npp)
        hs = _out_proj(o, sg, hs, w_o, nb)

    return (hp.reshape(bsz, seq, D_MODEL), hs.reshape(nb, 1, D_MODEL),
            jnp.stack(sp), jnp.stack(ss),
            kp.reshape(bsz, seq, B_HEADS, B_DH), ks.reshape(nb, 1, B_HEADS, B_DH),
            vp.reshape(bsz, seq, B_HEADS, B_DH), vs.reshape(nb, 1, B_HEADS, B_DH),
            lfp.reshape(bsz, seq, B_HEADS), lfs.reshape(nb, 1, B_HEADS))
```

```python
import functools

import math

import jax
import jax.numpy as jnp
import numpy as np
from jax import lax
from jax.experimental import pallas as pl
from jax.experimental.pallas import tpu as pltpu

F32 = jnp.float32
BF16 = jnp.bfloat16

D_MODEL = 1024
A_HEADS = 8
A_DK = 128
A_DV = 128
B_HEADS = 16
B_DH = 64
RMS_EPS = 1e-6
ATTN_SCALE = B_DH ** -0.5
LOG2E = math.log2(math.e)

LANES = 128
SUBLANES = 8
REC_HEADS = 4
REC_CHUNK = 128
VMEM_LIMIT = 48 * 1024 * 1024

_NT = (((1,), (1,)), ((), ()))
_TN = (((0,), (0,)), ((), ()))


def _cparams(sem):
    return pltpu.CompilerParams(dimension_semantics=sem, vmem_limit_bytes=VMEM_LIMIT)


def _dot(a, b):
    return jnp.dot(a, b, preferred_element_type=F32)


def _rms_rows(x, g):
    ms = jnp.mean(x * x, axis=-1, keepdims=True)
    return x * lax.rsqrt(ms + RMS_EPS) * g


def _split3(x):
    hi = x.astype(BF16)
    r1 = x - hi.astype(F32)
    mid = r1.astype(BF16)
    lo = (r1 - mid.astype(F32)).astype(BF16)
    return hi, mid, lo


def _dot3(x, sel):
    hi, mid, lo = _split3(x)
    return _dot(hi, sel) + _dot(mid, sel) + _dot(lo, sel)


def _sel_dot3(sel, x):
    hi, mid, lo = _split3(x)
    return _dot(sel, hi) + _dot(sel, mid) + _dot(sel, lo)


def _sigmoid(x):
    return 1.0 / (1.0 + jnp.exp(-x))


def _dot2(x, sel):
    hi = x.astype(BF16)
    lo = (x - hi.astype(F32)).astype(BF16)
    return _dot(hi, sel) + _dot(lo, sel)


def _head_rms(x, ind, ind_t, gain):
    ssum = _dot2(x * x, ind)
    r = lax.rsqrt(ssum * (1.0 / B_DH) + RMS_EPS)
    return x * _dot2(r, ind_t) * gain


def _hgrn_proj_kernel(layer, x_ref, g_ref, w_ref, lbl_ref, q_ref, k_ref, lf_ref, v_ref, sg_ref):
    xn = _rms_rows(x_ref[...], g_ref[...]).astype(BF16)
    hk = A_HEADS * A_DK

    pq = _dot(xn, w_ref[:, 0:hk])
    q_ref[...] = pq * _sigmoid(pq)

    fz = _dot(xn, w_ref[:, hk:2 * hk])
    ez = jnp.exp(-jnp.abs(fz))
    log_sig = jnp.minimum(fz, 0.0) - jnp.log(1.0 + ez)
    sig_neg = jnp.where(fz >= 0.0, ez, 1.0) / (1.0 + ez)
    if layer == 0:
        lf_ref[...] = log_sig
        k_ref[...] = sig_neg
    else:
        lbl = lbl_ref[...]
        e = jnp.exp(lbl - jnp.max(lbl, axis=0, keepdims=True))
        p = e / jnp.sum(e, axis=0, keepdims=True)
        csum = p[0:1, :]
        for i in range(1, layer + 1):
            csum = csum + p[i:i + 1, :]
        lb = csum - p[0:1, :]
        log_lb = jnp.log(lb)
        b = jnp.log1p(-lb) + log_sig
        lf_ref[...] = jnp.maximum(log_lb, b) + jnp.log(1.0 + jnp.exp(-jnp.abs(log_lb - b)))
        k_ref[...] = (1.0 - lb) * sig_neg

    v_ref[...] = _dot(xn, w_ref[:, 2 * hk:2 * hk + A_HEADS * A_DV])
    pg = _dot(xn, w_ref[:, 2 * hk + A_HEADS * A_DV:])
    sg_ref[...] = pg * _sigmoid(pg)


def _hgrn_proj(x, g, w_bf, lb_logits, layer, tm):
    m = x.shape[0]
    n_in = w_bf.shape[1]
    row = pl.BlockSpec((tm, D_MODEL), lambda i: (i, 0))
    full = lambda shape: pl.BlockSpec(shape, lambda i: (0, 0), pipeline_mode=pl.Buffered(1))
    out = jax.ShapeDtypeStruct((m, D_MODEL), F32)
    return pl.pallas_call(
        functools.partial(_hgrn_proj_kernel, layer),
        grid=(m // tm,),
        in_specs=[row, full((1, D_MODEL)), full((D_MODEL, n_in)), full(lb_logits.shape)],
        out_specs=[row] * 5,
        out_shape=[out] * 5,
        compiler_params=_cparams(("parallel",)),
        name="hgrn_proj",
    )(x, g.reshape(1, D_MODEL), w_bf, lb_logits)


def _rec_chunk(q, k, v, lf, s_t):
    c = q.shape[0]
    nt = c // SUBLANES
    tiles = lambda x: [x[SUBLANES * i:SUBLANES * (i + 1), :] for i in range(nt)]
    cat = lambda ts: jnp.concatenate(ts, axis=0).astype(BF16)
    nt_dot = lambda a, b: lax.dot_general(a, b, _NT, preferred_element_type=F32)
    xor = lax.broadcasted_iota(jnp.int32, (c, c), 0) ^ lax.broadcasted_iota(jnp.int32, (c, c), 1)
    att = jnp.where(xor == 0, nt_dot(q.astype(BF16), k.astype(BF16)), 0.0)
    q_t, k_t = tiles(q), tiles(k)
    pre = tiles(lf)
    tot = tiles(lf)

    row = lax.broadcasted_iota(jnp.int32, (SUBLANES, LANES), 0)
    m = 1
    while m < SUBLANES:
        right = (row & m) != 0
        qt, kt = [], []
        for i in range(nt):
            e = jnp.exp(jnp.where(right, pre[i], tot[i] - pre[i]))
            qt.append(jnp.where(right, q_t[i] * e, 0.0))
            kt.append(jnp.where(right, 0.0, k_t[i] * e))
            down = pltpu.roll(tot[i], m, 0)
            up = pltpu.roll(tot[i], SUBLANES - m, 0)
            pre[i] = pre[i] + jnp.where(right, down, 0.0)
            tot[i] = tot[i] + jnp.where(right, down, up)
        att = att + jnp.where(xor < 2 * m, nt_dot(cat(qt), cat(kt)), 0.0)
        m *= 2

    zero = jnp.zeros((SUBLANES, LANES), F32)
    mv = 1
    while mv < nt:
        qt, kt = [zero] * nt, [zero] * nt
        for base in range(0, nt, 2 * mv):
            for j in range(mv):
                lt, rt = base + j, base + mv + j
                kt[lt] = k_t[lt] * jnp.exp(tot[lt] - pre[lt])
                qt[rt] = q_t[rt] * jnp.exp(pre[rt])
        a = nt_dot(cat(qt), cat(kt))
        att = att + (a if 2 * mv == nt else jnp.where(xor < 2 * mv * SUBLANES, a, 0.0))
        for base in range(0, nt, 2 * mv):
            left_total = tot[base]
            both = left_total + tot[base + mv]
            for j in range(mv):
                pre[base + mv + j] = pre[base + mv + j] + left_total
            for j in range(2 * mv):
                tot[base + j] = both
        mv *= 2

    vb = v.astype(BF16)
    qg = cat([q_t[i] * jnp.exp(pre[i]) for i in range(nt)])
    kh = cat([k_t[i] * jnp.exp(tot[i] - pre[i]) for i in range(nt)])
    o = nt_dot(qg, s_t.astype(BF16)) + _dot(att.astype(BF16), vb)
    s_new = s_t * jnp.exp(tot[0][0:1, :]) + lax.dot_general(vb, kh, _TN, preferred_element_type=F32)
    return o, s_new


def _hgrn_rec_kernel(n_chunks, nh, q_ref, k_ref, lf_ref, v_ref, s0_ref, gn_ref, o_ref, s_out_ref, st_ref):
    t = pl.program_id(2)

    @pl.when(t == 0)
    def _():
        for h in range(nh):
            st_ref[h] = s0_ref[0, h].T

    def body(ci, carry):
        rows = pl.ds(pl.multiple_of(ci * REC_CHUNK, REC_CHUNK), REC_CHUNK)
        for h in range(nh):
            cols = slice(h * A_DK, (h + 1) * A_DK)
            o, s_new = _rec_chunk(q_ref[rows, cols], k_ref[rows, cols], v_ref[rows, cols],
                                  lf_ref[rows, cols], st_ref[h])
            st_ref[h] = s_new
            o_ref[rows, cols] = _rms_rows(o, gn_ref[...])
        return carry

    lax.fori_loop(0, n_chunks, body, 0)

    @pl.when(t == pl.num_programs(2) - 1)
    def _():
        for h in range(nh):
            s_out_ref[0, h] = st_ref[h].T


def _hgrn_rec(q, k, lf, v, s0, g_norm, bsz, seq, tb, nh):
    n_t = seq // tb
    blk = pl.BlockSpec((tb, nh * A_DK), lambda b, h, t: (b * n_t + t, h))
    st = pl.BlockSpec((1, nh, A_DK, A_DV), lambda b, h, t: (b, h, 0, 0))
    return pl.pallas_call(
        functools.partial(_hgrn_rec_kernel, tb // REC_CHUNK, nh),
        grid=(bsz, A_HEADS // nh, n_t),
        in_specs=[blk, blk, blk, blk, st, pl.BlockSpec((1, A_DV), lambda b, h, t: (0, 0))],
        out_specs=[blk, st],
        out_shape=[jax.ShapeDtypeStruct(q.shape, F32), jax.ShapeDtypeStruct(s0.shape, F32)],
        scratch_shapes=[pltpu.VMEM((nh, A_DV, A_DK), F32)],
        compiler_params=_cparams(("parallel", "parallel", "arbitrary")),
        name="hgrn_rec",
    )(q, k, lf, v, s0, g_norm.reshape(1, A_DV))


def _hgrn_step_kernel(nb, qt_ref, kt_ref, lft_ref, v_ref, s_ref, gn_ref, o_ref, s_out_ref):
    qt = qt_ref[0]
    kt = kt_ref[0]
    ft = jnp.exp(lft_ref[0])
    for b in range(nb):
        col = lambda a: jnp.broadcast_to(a[:, b:b + 1], (A_DK, A_DV))
        s_new = col(ft) * s_ref[b, 0] + col(kt) * v_ref[b:b + 1, :]
        s_out_ref[b, 0] = s_new
        o = jnp.sum(col(qt) * s_new, axis=0, keepdims=True)
        o_ref[b:b + 1, :] = _rms_rows(o, gn_ref[...])


def _hgrn_step(q, k, lf, v, s, g_norm):
    nb = q.shape[0]
    tr = lambda a: a.reshape(nb, A_HEADS, A_DK).transpose(1, 2, 0)
    colspec = pl.BlockSpec((1, A_DK, nb), lambda h: (h, 0, 0))
    vspec = pl.BlockSpec((nb, A_DV), lambda h: (0, h))
    sspec = pl.BlockSpec((nb, 1, A_DK, A_DV), lambda h: (0, h, 0, 0))
    return pl.pallas_call(
        functools.partial(_hgrn_step_kernel, nb),
        grid=(A_HEADS,),
        in_specs=[colspec, colspec, colspec, vspec, sspec, pl.BlockSpec((1, A_DV), lambda h: (0, 0))],
        out_specs=[vspec, sspec],
        out_shape=[jax.ShapeDtypeStruct(v.shape, F32), jax.ShapeDtypeStruct(s.shape, F32)],
        compiler_params=_cparams(("parallel",)),
        name="hgrn_step",
    )(tr(q), tr(k), tr(lf), v, s, g_norm.reshape(1, A_DV))


def _out_proj_kernel(a_ref, sg_ref, x_ref, w_ref, y_ref):
    y_ref[...] = x_ref[...] + _dot((a_ref[...] * sg_ref[...]).astype(BF16), w_ref[...])


def _out_proj(a, sg, x, w_bf, tm):
    m = x.shape[0]
    row = pl.BlockSpec((tm, D_MODEL), lambda i: (i, 0))
    return pl.pallas_call(
        _out_proj_kernel,
        grid=(m // tm,),
        in_specs=[row, row, row, pl.BlockSpec(w_bf.shape, lambda i: (0, 0))],
        out_specs=row,
        out_shape=jax.ShapeDtypeStruct(x.shape, F32),
        compiler_params=_cparams(("parallel",)),
        name="out_proj",
    )(a, sg, x, w_bf)


BIAS_COLS = 6


def _bias_lane(h):
    return (B_DH if h % 2 == 0 else 0) + BIAS_COLS * (h // 2)


def _aug_constants():
    place_q = np.zeros((3, B_HEADS, LANES), np.float32)
    place_k = np.zeros((3, B_HEADS, LANES), np.float32)
    ones_q = np.zeros((1, LANES), np.float32)
    ones_k = np.zeros((1, LANES), np.float32)
    for h in range(B_HEADS):
        base = _bias_lane(h)
        for p in range(3):
            place_q[p, h, base + p] = 1.0
            ones_q[0, base + 3 + p] = 1.0
            ones_k[0, base + p] = 1.0
            place_k[p, h, base + 3 + p] = -1.0
    as_bf = lambda a: jnp.asarray(a, BF16)
    return as_bf(place_q), jnp.asarray(ones_q), as_bf(place_k), jnp.asarray(ones_k)


def _bias_columns(pieces, place_ref, ones_ref):
    out = ones_ref[...]
    for p in range(3):
        out = out + _dot(pieces[p], place_ref[p])
    return out


def _store_heads(x, other, o_ref):
    lane = lax.broadcasted_iota(jnp.int32, (x.shape[0], LANES), 1)
    for h in range(B_HEADS):
        own = (lane < B_DH) if h % 2 == 0 else (lane >= B_DH)
        xv = x[:, (h // 2) * LANES:(h // 2 + 1) * LANES]
        o_ref[0, h] = jnp.where(own, xv, other(h, lane)).astype(BF16)


def _shared_kv_kernel(x_ref, g_ref, wkv_ref, kn_ref, wf_ref, bf_ref, ind_ref, indt_ref,
                      place_ref, ones_ref,
                      k_ref, v_ref, lf_ref, chi_ref, cmid_ref, clo_ref, kaug_ref, vaug_ref, carry_ref):
    tm = x_ref.shape[0]
    u = _rms_rows(x_ref[...], g_ref[...]).astype(BF16)
    bw = B_HEADS * B_DH
    k = _head_rms(_dot(u, wkv_ref[:, 0:bw]), ind_ref[...], indt_ref[...], kn_ref[...])
    k_ref[...] = k
    v = _dot(u, wkv_ref[:, bw:])
    v_ref[...] = v
    z = _dot(u, wf_ref[...]) + bf_ref[...]
    lf = jnp.minimum(z, 0.0) - jnp.log(1.0 + jnp.exp(-jnp.abs(z)))
    lf_ref[...] = lf

    @pl.when(pl.program_id(1) == 0)
    def _():
        carry_ref[...] = jnp.zeros_like(carry_ref)

    lower = (lax.broadcasted_iota(jnp.int32, (tm, tm), 0)
             >= lax.broadcasted_iota(jnp.int32, (tm, tm), 1)).astype(BF16)
    c = _sel_dot3(lower, lf) + carry_ref[...]
    carry_ref[...] = c[tm - 1:tm, :]
    pieces = _split3(c * LOG2E)
    chi_ref[...], cmid_ref[...], clo_ref[...] = pieces
    bias = _bias_columns(pieces, place_ref, ones_ref)

    lane_row = lax.broadcasted_iota(jnp.int32, (1, LANES), 1)

    def own_bias(h, lane):
        offset = lane_row - _bias_lane(h)
        return bias * jnp.where((offset >= 0) & (offset < BIAS_COLS), 1.0, 0.0)

    def one_hot(h, lane):
        return jnp.where(lane_row == (B_DH if h % 2 == 0 else 0), 1.0, 0.0)

    _store_heads(k, own_bias, kaug_ref)
    _store_heads(v, one_hot, vaug_ref)


def _head_indicator():
    ind = (jnp.arange(B_HEADS * B_DH)[:, None] // B_DH == jnp.arange(B_HEADS)[None, :])
    return ind.astype(BF16), ind.T.astype(BF16)


def _shared_kv(x, g, wkv_bf, k_norm, wf_bf, b_fgate, bsz, seq, tm):
    m = x.shape[0]
    n_t = seq // tm
    ind, ind_t = _head_indicator()
    _, _, place_k, ones_k = _aug_constants()
    row = pl.BlockSpec((tm, D_MODEL), lambda b, t: (b * n_t + t, 0))
    hrow = pl.BlockSpec((tm, B_HEADS), lambda b, t: (b * n_t + t, 0))
    heads = pl.BlockSpec((1, B_HEADS, tm, LANES), lambda b, t: (b, 0, t, 0))
    full = lambda a: pl.BlockSpec(a.shape, lambda b, t: (0,) * a.ndim)
    consts = [g.reshape(1, D_MODEL), wkv_bf, jnp.tile(k_norm, B_HEADS).reshape(1, D_MODEL),
              wf_bf, b_fgate.reshape(1, B_HEADS), ind, ind_t, place_k, ones_k]
    big = jax.ShapeDtypeStruct((m, D_MODEL), F32)
    small = jax.ShapeDtypeStruct((m, B_HEADS), F32)
    piece = jax.ShapeDtypeStruct((m, B_HEADS), BF16)
    aug = jax.ShapeDtypeStruct((bsz, B_HEADS, seq, LANES), BF16)
    return pl.pallas_call(
        _shared_kv_kernel,
        grid=(bsz, n_t),
        in_specs=[row] + [full(a) for a in consts],
        out_specs=[row, row, hrow, hrow, hrow, hrow, heads, heads],
        out_shape=[big, big, small, piece, piece, piece, aug, aug],
        scratch_shapes=[pltpu.VMEM((1, B_HEADS), F32)],
        compiler_params=_cparams(("parallel", "arbitrary")),
        name="shared_kv",
    )(x, *consts)


def _fox_qg(x_ref, g_ref, w_ref, qn_ref, ind_ref, indt_ref):
    xn = _rms_rows(x_ref[...], g_ref[...]).astype(BF16)
    bw = B_HEADS * B_DH
    q = _head_rms(_dot(xn, w_ref[:, 0:bw]), ind_ref[...], indt_ref[...], qn_ref[...])
    pg = _dot(xn, w_ref[:, bw:])
    return q, pg * _sigmoid(pg)


def _fox_proj_kernel(x_ref, g_ref, w_ref, qn_ref, ind_ref, indt_ref, q_ref, sg_ref):
    q, sg = _fox_qg(x_ref, g_ref, w_ref, qn_ref, ind_ref, indt_ref)
    q_ref[...] = q * ATTN_SCALE
    sg_ref[...] = sg


def _fox_proj_aug_kernel(x_ref, g_ref, w_ref, qn_ref, ind_ref, indt_ref, chi_ref, cmid_ref, clo_ref,
                         place_ref, ones_ref, qaug_ref, sg_ref):
    q, sg = _fox_qg(x_ref, g_ref, w_ref, qn_ref, ind_ref, indt_ref)
    sg_ref[...] = sg
    pieces = (chi_ref[...], cmid_ref[...], clo_ref[...])
    bias = _bias_columns(pieces, place_ref, ones_ref)
    _store_heads(q * (ATTN_SCALE * LOG2E), lambda h, lane: bias, qaug_ref)


def _fox_proj(x, g, w_bf, q_norm, tm):
    m = x.shape[0]
    ind, ind_t = _head_indicator()
    row = pl.BlockSpec((tm, D_MODEL), lambda i: (i, 0))
    full = lambda a: pl.BlockSpec(a.shape, lambda i: (0,) * a.ndim)
    consts = [g.reshape(1, D_MODEL), w_bf, jnp.tile(q_norm, B_HEADS).reshape(1, D_MODEL), ind, ind_t]
    out = jax.ShapeDtypeStruct((m, D_MODEL), F32)
    return pl.pallas_call(
        _fox_proj_kernel,
        grid=(m // tm,),
        in_specs=[row] + [full(a) for a in consts],
        out_specs=[row, row],
        out_shape=[out, out],
        compiler_params=_cparams(("parallel",)),
        name="fox_proj",
    )(x, *consts)


def _fox_proj_aug(x, g, w_bf, q_norm, c_pieces, bsz, seq, tm):
    m = x.shape[0]
    n_t = seq // tm
    ind, ind_t = _head_indicator()
    place_q, ones_q, _, _ = _aug_constants()
    row = pl.BlockSpec((tm, D_MODEL), lambda b, t: (b * n_t + t, 0))
    hrow = pl.BlockSpec((tm, B_HEADS), lambda b, t: (b * n_t + t, 0))
    heads = pl.BlockSpec((1, B_HEADS, tm, LANES), lambda b, t: (b, 0, t, 0))
    full = lambda a: pl.BlockSpec(a.shape, lambda b, t: (0,) * a.ndim)
    consts = [g.reshape(1, D_MODEL), w_bf, jnp.tile(q_norm, B_HEADS).reshape(1, D_MODEL), ind, ind_t]
    return pl.pallas_call(
        _fox_proj_aug_kernel,
        grid=(bsz, n_t),
        in_specs=[row] + [full(a) for a in consts] + [hrow] * 3 + [full(place_q), full(ones_q)],
        out_specs=[heads, row],
        out_shape=[jax.ShapeDtypeStruct((bsz, B_HEADS, seq, LANES), BF16),
                   jax.ShapeDtypeStruct((m, D_MODEL), F32)],
        compiler_params=_cparams(("parallel", "parallel")),
        name="fox_proj_aug",
    )(x, *consts, *c_pieces, place_q, ones_q)


def _flash_kernel(tq, q_ref, k_ref, v_ref, o_ref, s_ref, m_ref, acc_ref):
    qi = pl.program_id(2)

    def block(ki):
        return pl.ds(pl.multiple_of(ki * tq, tq), tq)

    def put_scores(slot, ki):
        for h in range(2):
            s_ref[slot, h] = lax.dot_general(q_ref[0, h], k_ref[0, h, block(ki), :], _NT,
                                             preferred_element_type=F32)

    def consume(slot, ki, diagonal):
        for h in range(2):
            sh = s_ref[slot, h]
            if diagonal:
                keep = (lax.broadcasted_iota(jnp.int32, (tq, tq), 0)
                        >= lax.broadcasted_iota(jnp.int32, (tq, tq), 1))
                sh = jnp.where(keep, sh, -jnp.inf)
            m = m_ref[h]
            m_new = jnp.maximum(m, jnp.max(sh, axis=-1, keepdims=True))
            p = jnp.exp2(sh - m_new).astype(BF16)
            acc_ref[h] = acc_ref[h] * jnp.exp2(m - m_new) + _dot(p, v_ref[0, h, block(ki), :])
            m_ref[h] = m_new

    m_ref[...] = jnp.full(m_ref.shape, -jnp.inf, F32)
    acc_ref[...] = jnp.zeros(acc_ref.shape, F32)
    put_scores(0, 0)

    def body(j, carry):
        ki = 2 * j
        put_scores(1, ki + 1)
        consume(0, ki, False)
        put_scores(0, ki + 2)
        consume(1, ki + 1, False)
        return carry

    lax.fori_loop(0, qi // 2, body, 0)

    @pl.when(qi % 2 == 0)
    def _():
        consume(0, qi, True)

    @pl.when(qi % 2 == 1)
    def _():
        put_scores(1, qi)
        consume(0, qi - 1, False)
        consume(1, qi, True)

    lane = lax.broadcasted_iota(jnp.int32, (tq, LANES), 1)
    acc0 = acc_ref[0]
    acc1 = acc_ref[1]
    o_ref[...] = jnp.where(lane < B_DH, acc0 / acc0[:, B_DH:B_DH + 1], acc1 / acc1[:, 0:1])


def _flash(q_aug, k_aug, v_aug, tq):
    bsz, _, seq, _ = q_aug.shape
    n_q = seq // tq
    qspec = pl.BlockSpec((1, 2, tq, LANES), lambda b, hp, i: (b, hp, i, 0))
    kvspec = pl.BlockSpec((1, 2, seq, LANES), lambda b, hp, i: (b, hp, 0, 0))
    return pl.pallas_call(
        functools.partial(_flash_kernel, tq),
        grid=(bsz, B_HEADS // 2, n_q),
        in_specs=[qspec, kvspec, kvspec],
        out_specs=pl.BlockSpec((tq, LANES), lambda b, hp, i: (b * n_q + i, hp)),
        out_shape=jax.ShapeDtypeStruct((bsz * seq, B_HEADS * B_DH), F32),
        scratch_shapes=[pltpu.VMEM((2, 2, tq, tq), F32), pltpu.VMEM((2, tq, 1), F32),
                        pltpu.VMEM((2, tq, LANES), F32)],
        compiler_params=_cparams(("parallel", "parallel", "arbitrary")),
        name="fox_flash",
    )(q_aug, k_aug, v_aug)


def _decode_kernel(npp, page, pt_ref, qs_ref, qbd_ref, kn_ref, vn_ref, lfn_ref, ind_ref, indt_ref, *rest):
    k_refs = rest[0:npp]
    v_refs = rest[npp:2 * npp]
    lf_refs = rest[2 * npp:3 * npp]
    o_ref = rest[3 * npp]
    m_ref, l_ref, acc_ref, carry_ref = rest[3 * npp + 1:]
    j = pl.program_id(1)
    ind = ind_ref[...]
    ind_t = indt_ref[...]

    @pl.when(j == 0)
    def _():
        prod = qs_ref[0].astype(BF16).astype(F32) * kn_ref[0].astype(BF16).astype(F32)
        m_ref[...] = _dot3(prod, ind)
        l_ref[...] = jnp.ones_like(l_ref)
        acc_ref[...] = vn_ref[0]
        carry_ref[...] = lfn_ref[0]

    upper = (lax.broadcasted_iota(jnp.int32, (page, page), 1)
             > lax.broadcasted_iota(jnp.int32, (page, page), 0)).astype(BF16)
    qbd = qbd_ref[0]
    carry = carry_ref[...]
    logits = []
    for i in range(npp):
        lf = lf_refs[i][0]
        sfx = _sel_dot3(upper, lf)
        s = _dot(k_refs[i][0], qbd)
        logits.append(s + sfx + carry)
        carry = carry + sfx[0:1, :] + lf[0:1, :]
    carry_ref[...] = carry

    m_old = m_ref[...]
    m_new = m_old
    for lg in logits:
        m_new = jnp.maximum(m_new, jnp.max(lg, axis=0, keepdims=True))
    fold = lambda a: jnp.sum(a.reshape(page // SUBLANES, SUBLANES, a.shape[-1]), axis=0)
    l_add = jnp.zeros((SUBLANES, B_HEADS), F32)
    acc_add = jnp.zeros((SUBLANES, B_HEADS * B_DH), F32)
    for i in range(npp):
        p = jnp.exp(logits[i] - m_new)
        l_add = l_add + fold(p)
        pexp = _dot(p.astype(BF16), ind_t)
        acc_add = acc_add + fold(pexp * v_refs[i][0].astype(F32))
    alpha = jnp.exp(m_old - m_new)
    m_ref[...] = m_new
    l_ref[...] = l_ref[...] * alpha + jnp.sum(l_add, axis=0, keepdims=True)
    acc_ref[...] = acc_ref[...] * _dot3(alpha, ind_t) + jnp.sum(acc_add, axis=0, keepdims=True)

    @pl.when(j == pl.num_programs(1) - 1)
    def _():
        o_ref[0] = acc_ref[...] / _dot3(l_ref[...], ind_t)


def _fox_attention_sample(qs, k_new, v_new, lf_new, cache_k, cache_v, cache_logf, page_table, npp):
    nb, n_pages = page_table.shape
    n_pool, page = cache_k.shape[:2]
    bw = B_HEADS * B_DH
    ind, ind_t = _head_indicator()
    qbd = (qs[:, :, None] * ind.astype(F32)[None]).astype(BF16)
    ck, cv = cache_k, cache_v
    n_steps = n_pages // npp

    def page_map(i):
        def index_map(b, j, pt):
            return (pt[b * n_pages + (n_pages - 1 - (j * npp + i))], 0, 0)
        return index_map

    per_seq = lambda width: pl.BlockSpec((1, 1, width), lambda b, j, pt: (b, 0, 0))
    full = lambda a: pl.BlockSpec(a.shape, lambda b, j, pt: (0,) * a.ndim)
    in_specs = [per_seq(bw), pl.BlockSpec((1, bw, B_HEADS), lambda b, j, pt: (b, 0, 0)),
                per_seq(bw), per_seq(bw), per_seq(B_HEADS), full(ind), full(ind_t)]
    in_specs += [pl.BlockSpec((1, page, bw), page_map(i)) for i in range(npp)]
    in_specs += [pl.BlockSpec((1, page, bw), page_map(i)) for i in range(npp)]
    in_specs += [pl.BlockSpec((1, page, B_HEADS), page_map(i)) for i in range(npp)]
    grid_spec = pltpu.PrefetchScalarGridSpec(
        num_scalar_prefetch=1,
        grid=(nb, n_steps),
        in_specs=in_specs,
        out_specs=per_seq(bw),
        scratch_shapes=[pltpu.VMEM((1, B_HEADS), F32), pltpu.VMEM((1, B_HEADS), F32),
                        pltpu.VMEM((1, bw), F32), pltpu.VMEM((1, B_HEADS), F32)],
    )
    r3 = lambda a: a.reshape(nb, 1, -1)
    o = pl.pallas_call(
        functools.partial(_decode_kernel, npp, page),
        grid_spec=grid_spec,
        out_shape=jax.ShapeDtypeStruct((nb, 1, bw), F32),
        compiler_params=_cparams(("parallel", "arbitrary")),
        name="fox_decode",
    )(page_table.reshape(-1), r3(qs), qbd, r3(k_new), r3(v_new), r3(lf_new), ind, ind_t,
      *([ck] * npp), *([cv] * npp), *([cache_logf] * npp))
    return o.reshape(nb, bw)


def _pick(n, pref):
    return pref if n % pref == 0 else n


def kernel(x_prompt, x_sample, state_hgrn, cache_k, cache_v, cache_logf, page_table,
           norm_in, w_in_a, lb_logits, g_norm_a, w_o_a, norm_kv, w_kv, k_norm, w_fgate, b_fgate,
           w_in_b, q_norm_b, w_o_b):
    bsz, seq, _ = x_prompt.shape
    nb = x_sample.shape[0]
    n_a = w_in_a.shape[0]
    n_b = w_in_b.shape[0]
    tm = _pick(seq, 512)
    tb = _pick(seq, 512)
    tq = _pick(seq, 512)
    npp = _pick(page_table.shape[1], 8)

    hp = x_prompt.reshape(bsz * seq, D_MODEL)
    hs = x_sample.reshape(nb, D_MODEL)
    zero_state = jnp.zeros((bsz, A_HEADS, A_DK, A_DV), F32)
    sp, ss = [], []
    for l in range(n_a):
        w_in = w_in_a[l].astype(BF16)
        w_o = w_o_a[l].astype(BF16)
        q, k, lf, v, sg = _hgrn_proj(hp, norm_in[l], w_in, lb_logits, l, tm)
        o, s_new = _hgrn_rec(q, k, lf, v, zero_state, g_norm_a[l], bsz, seq, tb, REC_HEADS)
        hp = _out_proj(o, sg, hp, w_o, tm)
        sp.append(s_new)
        q, k, lf, v, sg = _hgrn_proj(hs, norm_in[l], w_in, lb_logits, l, nb)
        o, s_new = _hgrn_step(q, k, lf, v, state_hgrn[l], g_norm_a[l])
        hs = _out_proj(o, sg, hs, w_o, nb)
        ss.append(s_new)

    wkv = w_kv.astype(BF16)
    wf = w_fgate.astype(BF16)
    kp, vp, lfp, *c_pieces, k_aug, v_aug = _shared_kv(hp, norm_kv, wkv, k_norm, wf, b_fgate, bsz, seq, tm)
    ks, vs, lfs, *_ = _shared_kv(hs, norm_kv, wkv, k_norm, wf, b_fgate, 1, nb, nb)

    n_pool, page = cache_k.shape[:2]
    ck = cache_k.reshape(n_pool, page, B_HEADS * B_DH).astype(BF16)
    cv = cache_v.reshape(n_pool, page, B_HEADS * B_DH).astype(BF16)
    for j in range(n_b):
        w_in = w_in_b[j].astype(BF16)
        w_o = w_o_b[j].astype(BF16)
        g = norm_in[n_a + j]
        q_aug, sg = _fox_proj_aug(hp, g, w_in, q_norm_b[j], c_pieces, bsz, seq, tm)
        o = _flash(q_aug, k_aug, v_aug, tq)
        hp = _out_proj(o, sg, hp, w_o, tm)
        qs, sg = _fox_proj(hs, g, w_in, q_norm_b[j], nb)
        o = _fox_attention_sample(qs, ks, vs, lfs, ck, cv, cache_logf, page_table, npp)
        hs = _out_proj(o, sg, hs, w_o, nb)

    return (hp.reshape(bsz, seq, D_MODEL), hs.reshape(nb, 1, D_MODEL),
            jnp.stack(sp), jnp.stack(ss),
            kp.reshape(bsz, seq, B_HEADS, B_DH), ks.reshape(nb, 1, B_HEADS, B_DH),
            vp.reshape(bsz, seq, B_HEADS, B_DH), vs.reshape(nb, 1, B_HEADS, B_DH),
            lfp.reshape(bsz, seq, B_HEADS), lfs.reshape(nb, 1, B_HEADS))
```

```python
import functools

import math

import jax
import jax.numpy as jnp
import numpy as np
from jax import lax
from jax.experimental import pallas as pl
from jax.experimental.pallas import tpu as pltpu

F32 = jnp.float32
BF16 = jnp.bfloat16

D_MODEL = 1024
A_HEADS = 8
A_DK = 128
A_DV = 128
B_HEADS = 16
B_DH = 64
RMS_EPS = 1e-6
ATTN_SCALE = B_DH ** -0.5
LOG2E = math.log2(math.e)

LANES = 128
SUBLANES = 8
REC_HEADS = 4
REC_CHUNK = 128
VMEM_LIMIT = 48 * 1024 * 1024

_NT = (((1,), (1,)), ((), ()))
_TN = (((0,), (0,)), ((), ()))


def _cparams(sem):
    return pltpu.CompilerParams(dimension_semantics=sem, vmem_limit_bytes=VMEM_LIMIT)


def _dot(a, b):
    return jnp.dot(a, b, preferred_element_type=F32)


def _rms_rows(x, g):
    ms = jnp.mean(x * x, axis=-1, keepdims=True)
    return x * lax.rsqrt(ms + RMS_EPS) * g


def _split3(x):
    hi = x.astype(BF16)
    r1 = x - hi.astype(F32)
    mid = r1.astype(BF16)
    lo = (r1 - mid.astype(F32)).astype(BF16)
    return hi, mid, lo


def _dot3(x, sel):
    hi, mid, lo = _split3(x)
    return _dot(hi, sel) + _dot(mid, sel) + _dot(lo, sel)


def _sel_dot3(sel, x):
    hi, mid, lo = _split3(x)
    return _dot(sel, hi) + _dot(sel, mid) + _dot(sel, lo)


def _sigmoid(x):
    return 1.0 / (1.0 + jnp.exp(-x))


def _dot2(x, sel):
    hi = x.astype(BF16)
    lo = (x - hi.astype(F32)).astype(BF16)
    return _dot(hi, sel) + _dot(lo, sel)


def _head_rms(x, ind, ind_t, gain):
    ssum = _dot2(x * x, ind)
    r = lax.rsqrt(ssum * (1.0 / B_DH) + RMS_EPS)
    return x * _dot2(r, ind_t) * gain


def _hgrn_proj_kernel(layer, x_ref, g_ref, w_ref, lbl_ref, q_ref, k_ref, lf_ref, v_ref, sg_ref):
    xn = _rms_rows(x_ref[...], g_ref[...]).astype(BF16)
    hk = A_HEADS * A_DK

    pq = _dot(xn, w_ref[:, 0:hk])
    q_ref[...] = pq * _sigmoid(pq)

    fz = _dot(xn, w_ref[:, hk:2 * hk])
    ez = jnp.exp(-jnp.abs(fz))
    log_sig = jnp.minimum(fz, 0.0) - jnp.log(1.0 + ez)
    sig_neg = jnp.where(fz >= 0.0, ez, 1.0) / (1.0 + ez)
    if layer == 0:
        lf_ref[...] = log_sig
        k_ref[...] = sig_neg
    else:
        lbl = lbl_ref[...]
        e = jnp.exp(lbl - jnp.max(lbl, axis=0, keepdims=True))
        p = e / jnp.sum(e, axis=0, keepdims=True)
        csum = p[0:1, :]
        for i in range(1, layer + 1):
            csum = csum + p[i:i + 1, :]
        lb = csum - p[0:1, :]
        log_lb = jnp.log(lb)
        b = jnp.log1p(-lb) + log_sig
        lf_ref[...] = jnp.maximum(log_lb, b) + jnp.log(1.0 + jnp.exp(-jnp.abs(log_lb - b)))
        k_ref[...] = (1.0 - lb) * sig_neg

    v_ref[...] = _dot(xn, w_ref[:, 2 * hk:2 * hk + A_HEADS * A_DV])
    pg = _dot(xn, w_ref[:, 2 * hk + A_HEADS * A_DV:])
    sg_ref[...] = pg * _sigmoid(pg)


def _hgrn_proj(x, g, w_bf, lb_logits, layer, tm):
    m = x.shape[0]
    n_in = w_bf.shape[1]
    row = pl.BlockSpec((tm, D_MODEL), lambda i: (i, 0))
    full = lambda shape: pl.BlockSpec(shape, lambda i: (0, 0), pipeline_mode=pl.Buffered(1))
    out = jax.ShapeDtypeStruct((m, D_MODEL), F32)
    return pl.pallas_call(
        functools.partial(_hgrn_proj_kernel, layer),
        grid=(m // tm,),
        in_specs=[row, full((1, D_MODEL)), full((D_MODEL, n_in)), full(lb_logits.shape)],
        out_specs=[row] * 5,
        out_shape=[out] * 5,
        compiler_params=_cparams(("parallel",)),
        name="hgrn_proj",
    )(x, g.reshape(1, D_MODEL), w_bf, lb_logits)


def _rec_chunk(q, k, v, lf, s_t):
    c = q.shape[0]
    nt = c // SUBLANES
    tiles = lambda x: [x[SUBLANES * i:SUBLANES * (i + 1), :] for i in range(nt)]
    cat = lambda ts: jnp.concatenate(ts, axis=0).astype(BF16)
    nt_dot = lambda a, b: lax.dot_general(a, b, _NT, preferred_element_type=F32)
    xor = lax.broadcasted_iota(jnp.int32, (c, c), 0) ^ lax.broadcasted_iota(jnp.int32, (c, c), 1)
    att = jnp.where(xor == 0, nt_dot(q.astype(BF16), k.astype(BF16)), 0.0)
    q_t, k_t = tiles(q), tiles(k)
    pre = tiles(lf)
    tot = tiles(lf)

    row = lax.broadcasted_iota(jnp.int32, (SUBLANES, LANES), 0)
    m = 1
    while m < SUBLANES:
        right = (row & m) != 0
        qt, kt = [], []
        for i in range(nt):
            e = jnp.exp(jnp.where(right, pre[i], tot[i] - pre[i]))
            qt.append(jnp.where(right, q_t[i] * e, 0.0))
            kt.append(jnp.where(right, 0.0, k_t[i] * e))
            down = pltpu.roll(tot[i], m, 0)
            up = pltpu.roll(tot[i], SUBLANES - m, 0)
            pre[i] = pre[i] + jnp.where(right, down, 0.0)
            tot[i] = tot[i] + jnp.where(right, down, up)
        att = att + jnp.where(xor < 2 * m, nt_dot(cat(qt), cat(kt)), 0.0)
        m *= 2

    zero = jnp.zeros((SUBLANES, LANES), F32)
    mv = 1
    while mv < nt:
        qt, kt = [zero] * nt, [zero] * nt
        for base in range(0, nt, 2 * mv):
            for j in range(mv):
                lt, rt = base + j, base + mv + j
                kt[lt] = k_t[lt] * jnp.exp(tot[lt] - pre[lt])
                qt[rt] = q_t[rt] * jnp.exp(pre[rt])
        a = nt_dot(cat(qt), cat(kt))
        att = att + (a if 2 * mv == nt else jnp.where(xor < 2 * mv * SUBLANES, a, 0.0))
        for base in range(0, nt, 2 * mv):
            left_total = tot[base]
            both = left_total + tot[base + mv]
            for j in range(mv):
                pre[base + mv + j] = pre[base + mv + j] + left_total
            for j in range(2 * mv):
                tot[base + j] = both
        mv *= 2

    vb = v.astype(BF16)
    qg = cat([q_t[i] * jnp.exp(pre[i]) for i in range(nt)])
    kh = cat([k_t[i] * jnp.exp(tot[i] - pre[i]) for i in range(nt)])
    o = nt_dot(qg, s_t.astype(BF16)) + _dot(att.astype(BF16), vb)
    s_new = s_t * jnp.exp(tot[0][0:1, :]) + lax.dot_general(vb, kh, _TN, preferred_element_type=F32)
    return o, s_new


def _hgrn_rec_kernel(n_chunks, nh, q_ref, k_ref, lf_ref, v_ref, s0_ref, gn_ref, o_ref, s_out_ref, st_ref):
    t = pl.program_id(2)

    @pl.when(t == 0)
    def _():
        for h in range(nh):
            st_ref[h] = s0_ref[0, h].T

    def body(ci, carry):
        rows = pl.ds(pl.multiple_of(ci * REC_CHUNK, REC_CHUNK), REC_CHUNK)
        for h in range(nh):
            cols = slice(h * A_DK, (h + 1) * A_DK)
            o, s_new = _rec_chunk(q_ref[rows, cols], k_ref[rows, cols], v_ref[rows, cols],
                                  lf_ref[rows, cols], st_ref[h])
            st_ref[h] = s_new
            o_ref[rows, cols] = _rms_rows(o, gn_ref[...])
        return carry

    lax.fori_loop(0, n_chunks, body, 0)

    @pl.when(t == pl.num_programs(2) - 1)
    def _():
        for h in range(nh):
            s_out_ref[0, h] = st_ref[h].T


def _hgrn_rec(q, k, lf, v, s0, g_norm, bsz, seq, tb, nh):
    n_t = seq // tb
    blk = pl.BlockSpec((tb, nh * A_DK), lambda b, h, t: (b * n_t + t, h))
    st = pl.BlockSpec((1, nh, A_DK, A_DV), lambda b, h, t: (b, h, 0, 0))
    return pl.pallas_call(
        functools.partial(_hgrn_rec_kernel, tb // REC_CHUNK, nh),
        grid=(bsz, A_HEADS // nh, n_t),
        in_specs=[blk, blk, blk, blk, st, pl.BlockSpec((1, A_DV), lambda b, h, t: (0, 0))],
        out_specs=[blk, st],
        out_shape=[jax.ShapeDtypeStruct(q.shape, F32), jax.ShapeDtypeStruct(s0.shape, F32)],
        scratch_shapes=[pltpu.VMEM((nh, A_DV, A_DK), F32)],
        compiler_params=_cparams(("parallel", "parallel", "arbitrary")),
        name="hgrn_rec",
    )(q, k, lf, v, s0, g_norm.reshape(1, A_DV))


def _hgrn_step_kernel(nb, qt_ref, kt_ref, lft_ref, v_ref, s_ref, gn_ref, o_ref, s_out_ref):
    qt = qt_ref[0]
    kt = kt_ref[0]
    ft = jnp.exp(lft_ref[0])
    for b in range(nb):
        col = lambda a: jnp.broadcast_to(a[:, b:b + 1], (A_DK, A_DV))
        s_new = col(ft) * s_ref[b, 0] + col(kt) * v_ref[b:b + 1, :]
        s_out_ref[b, 0] = s_new
        o = jnp.sum(col(qt) * s_new, axis=0, keepdims=True)
        o_ref[b:b + 1, :] = _rms_rows(o, gn_ref[...])


def _hgrn_step(q, k, lf, v, s, g_norm):
    nb = q.shape[0]
    tr = lambda a: a.reshape(nb, A_HEADS, A_DK).transpose(1, 2, 0)
    colspec = pl.BlockSpec((1, A_DK, nb), lambda h: (h, 0, 0))
    vspec = pl.BlockSpec((nb, A_DV), lambda h: (0, h))
    sspec = pl.BlockSpec((nb, 1, A_DK, A_DV), lambda h: (0, h, 0, 0))
    return pl.pallas_call(
        functools.partial(_hgrn_step_kernel, nb),
        grid=(A_HEADS,),
        in_specs=[colspec, colspec, colspec, vspec, sspec, pl.BlockSpec((1, A_DV), lambda h: (0, 0))],
        out_specs=[vspec, sspec],
        out_shape=[jax.ShapeDtypeStruct(v.shape, F32), jax.ShapeDtypeStruct(s.shape, F32)],
        compiler_params=_cparams(("parallel",)),
        name="hgrn_step",
    )(tr(q), tr(k), tr(lf), v, s, g_norm.reshape(1, A_DV))


def _out_proj_kernel(a_ref, sg_ref, x_ref, w_ref, y_ref):
    y_ref[...] = x_ref[...] + _dot((a_ref[...] * sg_ref[...]).astype(BF16), w_ref[...])


def _out_proj(a, sg, x, w_bf, tm):
    m = x.shape[0]
    row = pl.BlockSpec((tm, D_MODEL), lambda i: (i, 0))
    return pl.pallas_call(
        _out_proj_kernel,
        grid=(m // tm,),
        in_specs=[row, row, row, pl.BlockSpec(w_bf.shape, lambda i: (0, 0))],
        out_specs=row,
        out_shape=jax.ShapeDtypeStruct(x.shape, F32),
        compiler_params=_cparams(("parallel",)),
        name="out_proj",
    )(a, sg, x, w_bf)


BIAS_COLS = 6


def _bias_lane(h):
    return (B_DH if h % 2 == 0 else 0) + BIAS_COLS * (h // 2)


def _aug_constants():
    place_q = np.zeros((3, B_HEADS, LANES), np.float32)
    place_k = np.zeros((3, B_HEADS, LANES), np.float32)
    ones_q = np.zeros((1, LANES), np.float32)
    ones_k = np.zeros((1, LANES), np.float32)
    for h in range(B_HEADS):
        base = _bias_lane(h)
        for p in range(3):
            place_q[p, h, base + p] = 1.0
            ones_q[0, base + 3 + p] = 1.0
            ones_k[0, base + p] = 1.0
            place_k[p, h, base + 3 + p] = -1.0
    as_bf = lambda a: jnp.asarray(a, BF16)
    return as_bf(place_q), jnp.asarray(ones_q), as_bf(place_k), jnp.asarray(ones_k)


def _bias_columns(pieces, place_ref, ones_ref):
    out = ones_ref[...]
    for p in range(3):
        out = out + _dot(pieces[p], place_ref[p])
    return out


def _store_heads(x, other, o_ref):
    lane = lax.broadcasted_iota(jnp.int32, (x.shape[0], LANES), 1)
    for h in range(B_HEADS):
        own = (lane < B_DH) if h % 2 == 0 else (lane >= B_DH)
        xv = x[:, (h // 2) * LANES:(h // 2 + 1) * LANES]
        o_ref[0, h] = jnp.where(own, xv, other(h, lane)).astype(BF16)


def _shared_kv_kernel(x_ref, g_ref, wkv_ref, kn_ref, wf_ref, bf_ref, ind_ref, indt_ref,
                      place_ref, ones_ref,
                      k_ref, v_ref, lf_ref, chi_ref, cmid_ref, clo_ref, kaug_ref, vaug_ref, carry_ref):
    tm = x_ref.shape[0]
    u = _rms_rows(x_ref[...], g_ref[...]).astype(BF16)
    bw = B_HEADS * B_DH
    k = _head_rms(_dot(u, wkv_ref[:, 0:bw]), ind_ref[...], indt_ref[...], kn_ref[...])
    k_ref[...] = k
    v = _dot(u, wkv_ref[:, bw:])
    v_ref[...] = v
    z = _dot(u, wf_ref[...]) + bf_ref[...]
    lf = jnp.minimum(z, 0.0) - jnp.log(1.0 + jnp.exp(-jnp.abs(z)))
    lf_ref[...] = lf

    @pl.when(pl.program_id(1) == 0)
    def _():
        carry_ref[...] = jnp.zeros_like(carry_ref)

    lower = (lax.broadcasted_iota(jnp.int32, (tm, tm), 0)
             >= lax.broadcasted_iota(jnp.int32, (tm, tm), 1)).astype(BF16)
    c = _sel_dot3(lower, lf) + carry_ref[...]
    carry_ref[...] = c[tm - 1:tm, :]
    pieces = _split3(c * LOG2E)
    chi_ref[...], cmid_ref[...], clo_ref[...] = pieces
    bias = _bias_columns(pieces, place_ref, ones_ref)

    lane_row = lax.broadcasted_iota(jnp.int32, (1, LANES), 1)

    def own_bias(h, lane):
        offset = lane_row - _bias_lane(h)
        return bias * jnp.where((offset >= 0) & (offset < BIAS_COLS), 1.0, 0.0)

    def one_hot(h, lane):
        return jnp.where(lane_row == (B_DH if h % 2 == 0 else 0), 1.0, 0.0)

    _store_heads(k, own_bias, kaug_ref)
    _store_heads(v, one_hot, vaug_ref)


def _head_indicator():
    ind = (jnp.arange(B_HEADS * B_DH)[:, None] // B_DH == jnp.arange(B_HEADS)[None, :])
    return ind.astype(BF16), ind.T.astype(BF16)


def _shared_kv(x, g, wkv_bf, k_norm, wf_bf, b_fgate, bsz, seq, tm):
    m = x.shape[0]
    n_t = seq // tm
    ind, ind_t = _head_indicator()
    _, _, place_k, ones_k = _aug_constants()
    row = pl.BlockSpec((tm, D_MODEL), lambda b, t: (b * n_t + t, 0))
    hrow = pl.BlockSpec((tm, B_HEADS), lambda b, t: (b * n_t + t, 0))
    heads = pl.BlockSpec((1, B_HEADS, tm, LANES), lambda b, t: (b, 0, t, 0))
    full = lambda a: pl.BlockSpec(a.shape, lambda b, t: (0,) * a.ndim)
    consts = [g.reshape(1, D_MODEL), wkv_bf, jnp.tile(k_norm, B_HEADS).reshape(1, D_MODEL),
              wf_bf, b_fgate.reshape(1, B_HEADS), ind, ind_t, place_k, ones_k]
    big = jax.ShapeDtypeStruct((m, D_MODEL), F32)
    small = jax.ShapeDtypeStruct((m, B_HEADS), F32)
    piece = jax.ShapeDtypeStruct((m, B_HEADS), BF16)
    aug = jax.ShapeDtypeStruct((bsz, B_HEADS, seq, LANES), BF16)
    return pl.pallas_call(
        _shared_kv_kernel,
        grid=(bsz, n_t),
        in_specs=[row] + [full(a) for a in consts],
        out_specs=[row, row, hrow, hrow, hrow, hrow, heads, heads],
        out_shape=[big, big, small, piece, piece, piece, aug, aug],
        scratch_shapes=[pltpu.VMEM((1, B_HEADS), F32)],
        compiler_params=_cparams(("parallel", "arbitrary")),
        name="shared_kv",
    )(x, *consts)


def _fox_qg(x_ref, g_ref, w_ref, qn_ref, ind_ref, indt_ref):
    xn = _rms_rows(x_ref[...], g_ref[...]).astype(BF16)
    bw = B_HEADS * B_DH
    q = _head_rms(_dot(xn, w_ref[:, 0:bw]), ind_ref[...], indt_ref[...], qn_ref[...])
    pg = _dot(xn, w_ref[:, bw:])
    return q, pg * _sigmoid(pg)


def _fox_proj_kernel(x_ref, g_ref, w_ref, qn_ref, ind_ref, indt_ref, q_ref, sg_ref):
    q, sg = _fox_qg(x_ref, g_ref, w_ref, qn_ref, ind_ref, indt_ref)
    q_ref[...] = q * ATTN_SCALE
    sg_ref[...] = sg


def _fox_proj_aug_kernel(x_ref, g_ref, w_ref, qn_ref, ind_ref, indt_ref, chi_ref, cmid_ref, clo_ref,
                         place_ref, ones_ref, qaug_ref, sg_ref):
    q, sg = _fox_qg(x_ref, g_ref, w_ref, qn_ref, ind_ref, indt_ref)
    sg_ref[...] = sg
    pieces = (chi_ref[...], cmid_ref[...], clo_ref[...])
    bias = _bias_columns(pieces, place_ref, ones_ref)
    _store_heads(q * (ATTN_SCALE * LOG2E), lambda h, lane: bias, qaug_ref)


def _fox_proj(x, g, w_bf, q_norm, tm):
    m = x.shape[0]
    ind, ind_t = _head_indicator()
    row = pl.BlockSpec((tm, D_MODEL), lambda i: (i, 0))
    full = lambda a: pl.BlockSpec(a.shape, lambda i: (0,) * a.ndim)
    consts = [g.reshape(1, D_MODEL), w_bf, jnp.tile(q_norm, B_HEADS).reshape(1, D_MODEL), ind, ind_t]
    out = jax.ShapeDtypeStruct((m, D_MODEL), F32)
    return pl.pallas_call(
        _fox_proj_kernel,
        grid=(m // tm,),
        in_specs=[row] + [full(a) for a in consts],
        out_specs=[row, row],
        out_shape=[out, out],
        compiler_params=_cparams(("parallel",)),
        name="fox_proj",
    )(x, *consts)


def _fox_proj_aug(x, g, w_bf, q_norm, c_pieces, bsz, seq, tm):
    m = x.shape[0]
    n_t = seq // tm
    ind, ind_t = _head_indicator()
    place_q, ones_q, _, _ = _aug_constants()
    row = pl.BlockSpec((tm, D_MODEL), lambda b, t: (b * n_t + t, 0))
    hrow = pl.BlockSpec((tm, B_HEADS), lambda b, t: (b * n_t + t, 0))
    heads = pl.BlockSpec((1, B_HEADS, tm, LANES), lambda b, t: (b, 0, t, 0))
    full = lambda a: pl.BlockSpec(a.shape, lambda b, t: (0,) * a.ndim)
    consts = [g.reshape(1, D_MODEL), w_bf, jnp.tile(q_norm, B_HEADS).reshape(1, D_MODEL), ind, ind_t]
    return pl.pallas_call(
        _fox_proj_aug_kernel,
        grid=(bsz, n_t),
        in_specs=[row] + [full(a) for a in consts] + [hrow] * 3 + [full(place_q), full(ones_q)],
        out_specs=[heads, row],
        out_shape=[jax.ShapeDtypeStruct((bsz, B_HEADS, seq, LANES), BF16),
                   jax.ShapeDtypeStruct((m, D_MODEL), F32)],
        compiler_params=_cparams(("parallel", "parallel")),
        name="fox_proj_aug",
    )(x, *consts, *c_pieces, place_q, ones_q)


def _flash_kernel(tq, q_ref, k_ref, v_ref, o_ref, s_ref, m_ref, acc_ref):
    qi = pl.program_id(2)

    def block(ki):
        return pl.ds(pl.multiple_of(ki * tq, tq), tq)

    def put_scores(slot, ki):
        for h in range(2):
            s_ref[slot, h] = lax.dot_general(q_ref[0, h], k_ref[0, h, block(ki), :], _NT,
                                             preferred_element_type=F32)

    def consume(slot, ki, diagonal):
        for h in range(2):
            sh = s_ref[slot, h]
            if diagonal:
                keep = (lax.broadcasted_iota(jnp.int32, (tq, tq), 0)
                        >= lax.broadcasted_iota(jnp.int32, (tq, tq), 1))
                sh = jnp.where(keep, sh, -jnp.inf)
            m = m_ref[h]
            m_new = jnp.maximum(m, jnp.max(sh, axis=-1, keepdims=True))
            p = jnp.exp2(sh - m_new).astype(BF16)
            acc_ref[h] = acc_ref[h] * jnp.exp2(m - m_new) + _dot(p, v_ref[0, h, block(ki), :])
            m_ref[h] = m_new

    m_ref[...] = jnp.full(m_ref.shape, -jnp.inf, F32)
    acc_ref[...] = jnp.zeros(acc_ref.shape, F32)
    put_scores(0, 0)

    def body(j, carry):
        ki = 2 * j
        put_scores(1, ki + 1)
        consume(0, ki, False)
        put_scores(0, ki + 2)
        consume(1, ki + 1, False)
        return carry

    lax.fori_loop(0, qi // 2, body, 0)

    @pl.when(qi % 2 == 0)
    def _():
        consume(0, qi, True)

    @pl.when(qi % 2 == 1)
    def _():
        put_scores(1, qi)
        consume(0, qi - 1, False)
        consume(1, qi, True)

    lane = lax.broadcasted_iota(jnp.int32, (tq, LANES), 1)
    acc0 = acc_ref[0]
    acc1 = acc_ref[1]
    o_ref[...] = jnp.where(lane < B_DH, acc0 / acc0[:, B_DH:B_DH + 1], acc1 / acc1[:, 0:1])


def _flash(q_aug, k_aug, v_aug, tq):
    bsz, _, seq, _ = q_aug.shape
    n_q = seq // tq
    qspec = pl.BlockSpec((1, 2, tq, LANES), lambda b, hp, i: (b, hp, i, 0))
    kvspec = pl.BlockSpec((1, 2, seq, LANES), lambda b, hp, i: (b, hp, 0, 0))
    return pl.pallas_call(
        functools.partial(_flash_kernel, tq),
        grid=(bsz, B_HEADS // 2, n_q),
        in_specs=[qspec, kvspec, kvspec],
        out_specs=pl.BlockSpec((tq, LANES), lambda b, hp, i: (b * n_q + i, hp)),
        out_shape=jax.ShapeDtypeStruct((bsz * seq, B_HEADS * B_DH), F32),
        scratch_shapes=[pltpu.VMEM((2, 2, tq, tq), F32), pltpu.VMEM((2, tq, 1), F32),
                        pltpu.VMEM((2, tq, LANES), F32)],
        compiler_params=_cparams(("parallel", "parallel", "arbitrary")),
        name="fox_flash",
    )(q_aug, k_aug, v_aug)


def _decode_kernel(npp, pt_ref, qcol_ref, knew_ref, vnew_ref, lfnew_ref, indt_ref, *rest):
    kt_refs = rest[0:npp]
    vt_refs = rest[npp:2 * npp]
    lf_refs = rest[2 * npp:3 * npp]
    o_ref = rest[3 * npp]
    m_ref, l_ref, acc_ref, carry_ref = rest[3 * npp + 1:]
    j = pl.program_id(1)
    page = LANES
    later = (lax.broadcasted_iota(jnp.int32, (page, page), 0)
             > lax.broadcasted_iota(jnp.int32, (page, page), 1)).astype(BF16)

    def update(pages, only_first_lane):
        n = len(pages)
        lf_all = jnp.concatenate([lf for _, _, lf in pages], axis=0)
        sfx_all = _dot3(lf_all, later)
        carry = carry_ref[...]
        logits = []
        for i, (kt, _, lf) in enumerate(pages):
            rows = [jnp.sum(kt[h] * qcol_ref[0, h], axis=0, keepdims=True) for h in range(B_HEADS)]
            lg = jnp.concatenate(rows, axis=0) + sfx_all[i * B_HEADS:(i + 1) * B_HEADS] + carry
            if only_first_lane:
                lane = lax.broadcasted_iota(jnp.int32, (B_HEADS, page), 1)
                lg = jnp.where(lane == 0, lg, -jnp.inf)
            logits.append(lg)
            carry = carry + jnp.sum(lf, axis=-1, keepdims=True)
        carry_ref[...] = carry
        m_old = m_ref[...]
        m_new = m_old
        for lg in logits:
            m_new = jnp.maximum(m_new, jnp.max(lg, axis=-1, keepdims=True))
        alpha = jnp.exp(m_old - m_new)
        probs = [jnp.exp(lg - m_new) for lg in logits]
        l_new = l_ref[...] * alpha
        for p in probs:
            l_new = l_new + p
        l_ref[...] = l_new
        m_ref[...] = m_new
        for h in range(B_HEADS):
            a = acc_ref[h] * alpha[h:h + 1, :]
            for i in range(n):
                a = a + probs[i][h:h + 1, :] * pages[i][1][h]
            acc_ref[h] = a

    @pl.when(j == 0)
    def _():
        m_ref[...] = jnp.full(m_ref.shape, -jnp.inf, F32)
        l_ref[...] = jnp.zeros(l_ref.shape, F32)
        acc_ref[...] = jnp.zeros(acc_ref.shape, F32)
        carry_ref[...] = jnp.zeros(carry_ref.shape, F32)
        update([(knew_ref.at[0], vnew_ref.at[0], lfnew_ref[0])], True)

    update([(kt_refs[i].at[0], vt_refs[i].at[0], lf_refs[i][0]) for i in range(npp)], False)

    @pl.when(j == pl.num_programs(1) - 1)
    def _():
        ones = jnp.ones((SUBLANES, page), BF16)
        lane_sum = lambda x: sum(lax.dot_general(ones, piece, _NT, preferred_element_type=F32)
                                 for piece in _split3(x))[0:1, :]
        num = lane_sum(acc_ref[...].reshape(B_HEADS * B_DH, page))
        den = lane_sum(l_ref[...])
        o_ref[0] = num / _dot3(den, indt_ref[...])


def _positions_on_lanes(a, nb, page):
    a = a.reshape(nb, B_HEADS, -1, 1)
    return jnp.pad(a, ((0, 0), (0, 0), (0, 0), (0, page - 1)))


def _fox_attention_sample(qs, k_new, v_new, lf_new, cache_kt, cache_vt, cache_lft, page_table, npp):
    nb, n_pages = page_table.shape
    page = cache_kt.shape[-1]
    bw = B_HEADS * B_DH
    _, ind_t = _head_indicator()
    n_steps = n_pages // npp
    qcol = jnp.broadcast_to(qs.reshape(nb, B_HEADS, B_DH, 1), (nb, B_HEADS, B_DH, page))
    knew = _positions_on_lanes(k_new, nb, page)
    vnew = _positions_on_lanes(v_new, nb, page)
    lfnew = _positions_on_lanes(lf_new, nb, page).reshape(nb, B_HEADS, page)

    def page_map(i, ndim):
        def index_map(b, j, pt):
            return (pt[b * n_pages + (n_pages - 1 - (j * npp + i))],) + (0,) * (ndim - 1)
        return index_map

    tile = (1, B_HEADS, B_DH, page)
    per_seq_tile = pl.BlockSpec(tile, lambda b, j, pt: (b, 0, 0, 0))
    in_specs = [per_seq_tile, per_seq_tile, per_seq_tile,
                pl.BlockSpec((1, B_HEADS, page), lambda b, j, pt: (b, 0, 0)),
                pl.BlockSpec(ind_t.shape, lambda b, j, pt: (0, 0))]
    in_specs += [pl.BlockSpec(tile, page_map(i, 4)) for i in range(npp)]
    in_specs += [pl.BlockSpec(tile, page_map(i, 4)) for i in range(npp)]
    in_specs += [pl.BlockSpec((1, B_HEADS, page), page_map(i, 3)) for i in range(npp)]
    grid_spec = pltpu.PrefetchScalarGridSpec(
        num_scalar_prefetch=1,
        grid=(nb, n_steps),
        in_specs=in_specs,
        out_specs=pl.BlockSpec((1, 1, bw), lambda b, j, pt: (b, 0, 0)),
        scratch_shapes=[pltpu.VMEM((B_HEADS, 1), F32), pltpu.VMEM((B_HEADS, page), F32),
                        pltpu.VMEM((B_HEADS, B_DH, page), F32), pltpu.VMEM((B_HEADS, 1), F32)],
    )
    o = pl.pallas_call(
        functools.partial(_decode_kernel, npp),
        grid_spec=grid_spec,
        out_shape=jax.ShapeDtypeStruct((nb, 1, bw), F32),
        compiler_params=_cparams(("parallel", "arbitrary")),
        name="fox_decode",
    )(page_table.reshape(-1), qcol, knew, vnew, lfnew, ind_t,
      *([cache_kt] * npp), *([cache_vt] * npp), *([cache_lft] * npp))
    return o.reshape(nb, bw)


def _pick(n, pref):
    return pref if n % pref == 0 else n


def kernel(x_prompt, x_sample, state_hgrn, cache_k, cache_v, cache_logf, page_table,
           norm_in, w_in_a, lb_logits, g_norm_a, w_o_a, norm_kv, w_kv, k_norm, w_fgate, b_fgate,
           w_in_b, q_norm_b, w_o_b):
    bsz, seq, _ = x_prompt.shape
    nb = x_sample.shape[0]
    n_a = w_in_a.shape[0]
    n_b = w_in_b.shape[0]
    tm = _pick(seq, 512)
    tb = _pick(seq, 512)
    tq = _pick(seq, 512)
    npp = _pick(page_table.shape[1], 8)

    hp = x_prompt.reshape(bsz * seq, D_MODEL)
    hs = x_sample.reshape(nb, D_MODEL)
    zero_state = jnp.zeros((bsz, A_HEADS, A_DK, A_DV), F32)
    sp, ss = [], []
    for l in range(n_a):
        w_in = w_in_a[l].astype(BF16)
        w_o = w_o_a[l].astype(BF16)
        q, k, lf, v, sg = _hgrn_proj(hp, norm_in[l], w_in, lb_logits, l, tm)
        o, s_new = _hgrn_rec(q, k, lf, v, zero_state, g_norm_a[l], bsz, seq, tb, REC_HEADS)
        hp = _out_proj(o, sg, hp, w_o, tm)
        sp.append(s_new)
        q, k, lf, v, sg = _hgrn_proj(hs, norm_in[l], w_in, lb_logits, l, nb)
        o, s_new = _hgrn_step(q, k, lf, v, state_hgrn[l], g_norm_a[l])
        hs = _out_proj(o, sg, hs, w_o, nb)
        ss.append(s_new)

    wkv = w_kv.astype(BF16)
    wf = w_fgate.astype(BF16)
    kp, vp, lfp, *c_pieces, k_aug, v_aug = _shared_kv(hp, norm_kv, wkv, k_norm, wf, b_fgate, bsz, seq, tm)
    ks, vs, lfs, *_ = _shared_kv(hs, norm_kv, wkv, k_norm, wf, b_fgate, 1, nb, nb)

    ck = jnp.transpose(cache_k, (0, 2, 3, 1))
    cv = jnp.transpose(cache_v, (0, 2, 3, 1))
    clf = jnp.transpose(cache_logf, (0, 2, 1))
    for j in range(n_b):
        w_in = w_in_b[j].astype(BF16)
        w_o = w_o_b[j].astype(BF16)
        g = norm_in[n_a + j]
        q_aug, sg = _fox_proj_aug(hp, g, w_in, q_norm_b[j], c_pieces, bsz, seq, tm)
        o = _flash(q_aug, k_aug, v_aug, tq)
        hp = _out_proj(o, sg, hp, w_o, tm)
        qs, sg = _fox_proj(hs, g, w_in, q_norm_b[j], nb)
        o = _fox_attention_sample(qs, ks, vs, lfs, ck, cv, clf, page_table, npp)
        hs = _out_proj(o, sg, hs, w_o, nb)

    return (hp.reshape(bsz, seq, D_MODEL), hs.reshape(nb, 1, D_MODEL),
            jnp.stack(sp), jnp.stack(ss),
            kp.reshape(bsz, seq, B_HEADS, B_DH), ks.reshape(nb, 1, B_HEADS, B_DH),
            vp.reshape(bsz, seq, B_HEADS, B_DH), vs.reshape(nb, 1, B_HEADS, B_DH),
            lfp.reshape(bsz, seq, B_HEADS), lfs.reshape(nb, 1, B_HEADS))
```

```python
import functools

import math

import jax
import jax.numpy as jnp
import numpy as np
from jax import lax
from jax.experimental import pallas as pl
from jax.experimental.pallas import tpu as pltpu

F32 = jnp.float32
BF16 = jnp.bfloat16

D_MODEL = 1024
A_HEADS = 8
A_DK = 128
A_DV = 128
B_HEADS = 16
B_DH = 64
RMS_EPS = 1e-6
ATTN_SCALE = B_DH ** -0.5
LOG2E = math.log2(math.e)

LANES = 128
SUBLANES = 8
REC_HEADS = 4
VT_ROWS = 80
REC_CHUNK = 128
VMEM_LIMIT = 48 * 1024 * 1024

_NT = (((1,), (1,)), ((), ()))
_TN = (((0,), (0,)), ((), ()))


def _cparams(sem):
    return pltpu.CompilerParams(dimension_semantics=sem, vmem_limit_bytes=VMEM_LIMIT)


def _dot(a, b):
    return jnp.dot(a, b, preferred_element_type=F32)


def _rms_rows(x, g):
    ms = jnp.mean(x * x, axis=-1, keepdims=True)
    return x * lax.rsqrt(ms + RMS_EPS) * g


def _split3(x):
    hi = x.astype(BF16)
    r1 = x - hi.astype(F32)
    mid = r1.astype(BF16)
    lo = (r1 - mid.astype(F32)).astype(BF16)
    return hi, mid, lo


def _dot3(x, sel):
    hi, mid, lo = _split3(x)
    return _dot(hi, sel) + _dot(mid, sel) + _dot(lo, sel)


def _sel_dot3(sel, x):
    hi, mid, lo = _split3(x)
    return _dot(sel, hi) + _dot(sel, mid) + _dot(sel, lo)


def _sigmoid(x):
    return 1.0 / (1.0 + jnp.exp(-x))


def _dot2(x, sel):
    hi = x.astype(BF16)
    lo = (x - hi.astype(F32)).astype(BF16)
    return _dot(hi, sel) + _dot(lo, sel)


def _head_rms(x, ind, ind_t, gain):
    ssum = _dot2(x * x, ind)
    r = lax.rsqrt(ssum * (1.0 / B_DH) + RMS_EPS)
    return x * _dot2(r, ind_t) * gain


def _hgrn_proj_kernel(layer, x_ref, g_ref, w_ref, lbl_ref, q_ref, k_ref, lf_ref, v_ref, sg_ref):
    xn = _rms_rows(x_ref[...], g_ref[...]).astype(BF16)
    hk = A_HEADS * A_DK

    pq = _dot(xn, w_ref[:, 0:hk])
    q_ref[...] = pq * _sigmoid(pq)

    fz = _dot(xn, w_ref[:, hk:2 * hk])
    ez = jnp.exp(-jnp.abs(fz))
    log_sig = jnp.minimum(fz, 0.0) - jnp.log(1.0 + ez)
    sig_neg = jnp.where(fz >= 0.0, ez, 1.0) / (1.0 + ez)
    if layer == 0:
        lf_ref[...] = log_sig
        k_ref[...] = sig_neg
    else:
        lbl = lbl_ref[...]
        e = jnp.exp(lbl - jnp.max(lbl, axis=0, keepdims=True))
        p = e / jnp.sum(e, axis=0, keepdims=True)
        csum = p[0:1, :]
        for i in range(1, layer + 1):
            csum = csum + p[i:i + 1, :]
        lb = csum - p[0:1, :]
        log_lb = jnp.log(lb)
        b = jnp.log1p(-lb) + log_sig
        lf_ref[...] = jnp.maximum(log_lb, b) + jnp.log(1.0 + jnp.exp(-jnp.abs(log_lb - b)))
        k_ref[...] = (1.0 - lb) * sig_neg

    v_ref[...] = _dot(xn, w_ref[:, 2 * hk:2 * hk + A_HEADS * A_DV])
    pg = _dot(xn, w_ref[:, 2 * hk + A_HEADS * A_DV:])
    sg_ref[...] = pg * _sigmoid(pg)


def _hgrn_proj(x, g, w_bf, lb_logits, layer, tm):
    m = x.shape[0]
    n_in = w_bf.shape[1]
    row = pl.BlockSpec((tm, D_MODEL), lambda i: (i, 0))
    full = lambda shape: pl.BlockSpec(shape, lambda i: (0, 0), pipeline_mode=pl.Buffered(1))
    out = jax.ShapeDtypeStruct((m, D_MODEL), F32)
    return pl.pallas_call(
        functools.partial(_hgrn_proj_kernel, layer),
        grid=(m // tm,),
        in_specs=[row, full((1, D_MODEL)), full((D_MODEL, n_in)), full(lb_logits.shape)],
        out_specs=[row] * 5,
        out_shape=[out] * 5,
        compiler_params=_cparams(("parallel",)),
        name="hgrn_proj",
    )(x, g.reshape(1, D_MODEL), w_bf, lb_logits)


def _rec_chunk(q, k, v, lf, s_t):
    c = q.shape[0]
    nt = c // SUBLANES
    tiles = lambda x: [x[SUBLANES * i:SUBLANES * (i + 1), :] for i in range(nt)]
    cat = lambda ts: jnp.concatenate(ts, axis=0).astype(BF16)
    nt_dot = lambda a, b: lax.dot_general(a, b, _NT, preferred_element_type=F32)
    xor = lax.broadcasted_iota(jnp.int32, (c, c), 0) ^ lax.broadcasted_iota(jnp.int32, (c, c), 1)
    att = jnp.where(xor == 0, nt_dot(q.astype(BF16), k.astype(BF16)), 0.0)
    q_t, k_t = tiles(q), tiles(k)
    pre = tiles(lf)
    tot = tiles(lf)

    row = lax.broadcasted_iota(jnp.int32, (SUBLANES, LANES), 0)
    m = 1
    while m < SUBLANES:
        right = (row & m) != 0
        qt, kt = [], []
        for i in range(nt):
            e = jnp.exp(jnp.where(right, pre[i], tot[i] - pre[i]))
            qt.append(jnp.where(right, q_t[i] * e, 0.0))
            kt.append(jnp.where(right, 0.0, k_t[i] * e))
            down = pltpu.roll(tot[i], m, 0)
            up = pltpu.roll(tot[i], SUBLANES - m, 0)
            pre[i] = pre[i] + jnp.where(right, down, 0.0)
            tot[i] = tot[i] + jnp.where(right, down, up)
        att = att + jnp.where(xor < 2 * m, nt_dot(cat(qt), cat(kt)), 0.0)
        m *= 2

    zero = jnp.zeros((SUBLANES, LANES), F32)
    mv = 1
    while mv < nt:
        qt, kt = [zero] * nt, [zero] * nt
        for base in range(0, nt, 2 * mv):
            for j in range(mv):
                lt, rt = base + j, base + mv + j
                kt[lt] = k_t[lt] * jnp.exp(tot[lt] - pre[lt])
                qt[rt] = q_t[rt] * jnp.exp(pre[rt])
        a = nt_dot(cat(qt), cat(kt))
        att = att + (a if 2 * mv == nt else jnp.where(xor < 2 * mv * SUBLANES, a, 0.0))
        for base in range(0, nt, 2 * mv):
            left_total = tot[base]
            both = left_total + tot[base + mv]
            for j in range(mv):
                pre[base + mv + j] = pre[base + mv + j] + left_total
            for j in range(2 * mv):
                tot[base + j] = both
        mv *= 2

    vb = v.astype(BF16)
    qg = cat([q_t[i] * jnp.exp(pre[i]) for i in range(nt)])
    kh = cat([k_t[i] * jnp.exp(tot[i] - pre[i]) for i in range(nt)])
    o = nt_dot(qg, s_t.astype(BF16)) + _dot(att.astype(BF16), vb)
    s_new = s_t * jnp.exp(tot[0][0:1, :]) + lax.dot_general(vb, kh, _TN, preferred_element_type=F32)
    return o, s_new


def _hgrn_rec_kernel(n_chunks, nh, q_ref, k_ref, lf_ref, v_ref, s0_ref, gn_ref, o_ref, s_out_ref, st_ref):
    t = pl.program_id(2)

    @pl.when(t == 0)
    def _():
        for h in range(nh):
            st_ref[h] = s0_ref[0, h].T

    def body(ci, carry):
        rows = pl.ds(pl.multiple_of(ci * REC_CHUNK, REC_CHUNK), REC_CHUNK)
        for h in range(nh):
            cols = slice(h * A_DK, (h + 1) * A_DK)
            o, s_new = _rec_chunk(q_ref[rows, cols], k_ref[rows, cols], v_ref[rows, cols],
                                  lf_ref[rows, cols], st_ref[h])
            st_ref[h] = s_new
            o_ref[rows, cols] = _rms_rows(o, gn_ref[...])
        return carry

    lax.fori_loop(0, n_chunks, body, 0)

    @pl.when(t == pl.num_programs(2) - 1)
    def _():
        for h in range(nh):
            s_out_ref[0, h] = st_ref[h].T


def _hgrn_rec(q, k, lf, v, s0, g_norm, bsz, seq, tb, nh):
    n_t = seq // tb
    blk = pl.BlockSpec((tb, nh * A_DK), lambda b, h, t: (b * n_t + t, h))
    st = pl.BlockSpec((1, nh, A_DK, A_DV), lambda b, h, t: (b, h, 0, 0))
    return pl.pallas_call(
        functools.partial(_hgrn_rec_kernel, tb // REC_CHUNK, nh),
        grid=(bsz, A_HEADS // nh, n_t),
        in_specs=[blk, blk, blk, blk, st, pl.BlockSpec((1, A_DV), lambda b, h, t: (0, 0))],
        out_specs=[blk, st],
        out_shape=[jax.ShapeDtypeStruct(q.shape, F32), jax.ShapeDtypeStruct(s0.shape, F32)],
        scratch_shapes=[pltpu.VMEM((nh, A_DV, A_DK), F32)],
        compiler_params=_cparams(("parallel", "parallel", "arbitrary")),
        name="hgrn_rec",
    )(q, k, lf, v, s0, g_norm.reshape(1, A_DV))


def _hgrn_step_kernel(nb, qt_ref, kt_ref, lft_ref, v_ref, s_ref, gn_ref, o_ref, s_out_ref):
    qt = qt_ref[0]
    kt = kt_ref[0]
    ft = jnp.exp(lft_ref[0])
    for b in range(nb):
        col = lambda a: jnp.broadcast_to(a[:, b:b + 1], (A_DK, A_DV))
        s_new = col(ft) * s_ref[b, 0] + col(kt) * v_ref[b:b + 1, :]
        s_out_ref[b, 0] = s_new
        o = jnp.sum(col(qt) * s_new, axis=0, keepdims=True)
        o_ref[b:b + 1, :] = _rms_rows(o, gn_ref[...])


def _hgrn_step(q, k, lf, v, s, g_norm):
    nb = q.shape[0]
    tr = lambda a: a.reshape(nb, A_HEADS, A_DK).transpose(1, 2, 0)
    colspec = pl.BlockSpec((1, A_DK, nb), lambda h: (h, 0, 0))
    vspec = pl.BlockSpec((nb, A_DV), lambda h: (0, h))
    sspec = pl.BlockSpec((nb, 1, A_DK, A_DV), lambda h: (0, h, 0, 0))
    return pl.pallas_call(
        functools.partial(_hgrn_step_kernel, nb),
        grid=(A_HEADS,),
        in_specs=[colspec, colspec, colspec, vspec, sspec, pl.BlockSpec((1, A_DV), lambda h: (0, 0))],
        out_specs=[vspec, sspec],
        out_shape=[jax.ShapeDtypeStruct(v.shape, F32), jax.ShapeDtypeStruct(s.shape, F32)],
        compiler_params=_cparams(("parallel",)),
        name="hgrn_step",
    )(tr(q), tr(k), tr(lf), v, s, g_norm.reshape(1, A_DV))


def _out_proj_kernel(a_ref, sg_ref, x_ref, w_ref, y_ref):
    y_ref[...] = x_ref[...] + _dot((a_ref[...] * sg_ref[...]).astype(BF16), w_ref[...])


def _out_proj(a, sg, x, w_bf, tm):
    m = x.shape[0]
    row = pl.BlockSpec((tm, D_MODEL), lambda i: (i, 0))
    return pl.pallas_call(
        _out_proj_kernel,
        grid=(m // tm,),
        in_specs=[row, row, row, pl.BlockSpec(w_bf.shape, lambda i: (0, 0))],
        out_specs=row,
        out_shape=jax.ShapeDtypeStruct(x.shape, F32),
        compiler_params=_cparams(("parallel",)),
        name="out_proj",
    )(a, sg, x, w_bf)


BIAS_COLS = 6


def _bias_lane(h):
    return (B_DH if h % 2 == 0 else 0) + BIAS_COLS * (h // 2)


def _aug_constants():
    place_q = np.zeros((3, B_HEADS, LANES), np.float32)
    place_k = np.zeros((3, B_HEADS, LANES), np.float32)
    ones_q = np.zeros((1, LANES), np.float32)
    ones_k = np.zeros((1, LANES), np.float32)
    for h in range(B_HEADS):
        base = _bias_lane(h)
        for p in range(3):
            place_q[p, h, base + p] = 1.0
            ones_q[0, base + 3 + p] = 1.0
            ones_k[0, base + p] = 1.0
            place_k[p, h, base + 3 + p] = -1.0
    as_bf = lambda a: jnp.asarray(a, BF16)
    return as_bf(place_q), jnp.asarray(ones_q), as_bf(place_k), jnp.asarray(ones_k)


def _bias_columns(pieces, place_ref, ones_ref):
    out = ones_ref[...]
    for p in range(3):
        out = out + _dot(pieces[p], place_ref[p])
    return out


def _store_heads(x, other, o_ref):
    lane = lax.broadcasted_iota(jnp.int32, (x.shape[0], LANES), 1)
    for h in range(B_HEADS):
        own = (lane < B_DH) if h % 2 == 0 else (lane >= B_DH)
        xv = x[:, (h // 2) * LANES:(h // 2 + 1) * LANES]
        o_ref[0, h] = jnp.where(own, xv, other(h, lane)).astype(BF16)


def _shared_kv_kernel(prompt, x_ref, g_ref, wkv_ref, kn_ref, wf_ref, bf_ref, ind_ref, indt_ref,
                      place_ref, ones_ref, k_ref, v_ref, lf_ref, *rest):
    tm = x_ref.shape[0]
    u = _rms_rows(x_ref[...], g_ref[...]).astype(BF16)
    bw = B_HEADS * B_DH
    k = _head_rms(_dot(u, wkv_ref[:, 0:bw]), ind_ref[...], indt_ref[...], kn_ref[...])
    v = _dot(u, wkv_ref[:, bw:])
    z = _dot(u, wf_ref[...]) + bf_ref[...]
    lf = jnp.minimum(z, 0.0) - jnp.log(1.0 + jnp.exp(-jnp.abs(z)))
    lf_ref[...] = lf
    if not prompt:
        k_ref[...] = k
        v_ref[...] = v
        return

    chi_ref, cmid_ref, clo_ref, kaug_ref, vtaug_ref, carry_ref = rest
    tail = jnp.where(lax.broadcasted_iota(jnp.int32, (VT_ROWS - B_DH, tm), 0) == 0, 1.0, 0.0).astype(BF16)
    for j in range(B_HEADS // 2):
        cols = slice(j * LANES, (j + 1) * LANES)
        k_ref[0, cols, :] = k[:, cols].T
        v_pair = v[:, cols].T
        v_ref[0, cols, :] = v_pair
        for half in range(2):
            vtaug_ref[0, 2 * j + half, 0:B_DH, :] = v_pair[half * B_DH:(half + 1) * B_DH, :].astype(BF16)
            vtaug_ref[0, 2 * j + half, B_DH:VT_ROWS, :] = tail

    @pl.when(pl.program_id(1) == 0)
    def _():
        carry_ref[...] = jnp.zeros_like(carry_ref)

    lower = (lax.broadcasted_iota(jnp.int32, (tm, tm), 0)
             >= lax.broadcasted_iota(jnp.int32, (tm, tm), 1)).astype(BF16)
    c = _sel_dot3(lower, lf) + carry_ref[...]
    carry_ref[...] = c[tm - 1:tm, :]
    pieces = _split3(c * LOG2E)
    chi_ref[...], cmid_ref[...], clo_ref[...] = pieces
    bias = _bias_columns(pieces, place_ref, ones_ref)

    lane_row = lax.broadcasted_iota(jnp.int32, (1, LANES), 1)

    def own_bias(h, lane):
        offset = lane_row - _bias_lane(h)
        return bias * jnp.where((offset >= 0) & (offset < BIAS_COLS), 1.0, 0.0)

    _store_heads(k, own_bias, kaug_ref)


def _head_indicator():
    ind = (jnp.arange(B_HEADS * B_DH)[:, None] // B_DH == jnp.arange(B_HEADS)[None, :])
    return ind.astype(BF16), ind.T.astype(BF16)


def _shared_kv(x, g, wkv_bf, k_norm, wf_bf, b_fgate, bsz, seq, tm, prompt):
    m = x.shape[0]
    n_t = seq // tm
    bw = B_HEADS * B_DH
    ind, ind_t = _head_indicator()
    _, _, place_k, ones_k = _aug_constants()
    row = pl.BlockSpec((tm, D_MODEL), lambda b, t: (b * n_t + t, 0))
    hrow = pl.BlockSpec((tm, B_HEADS), lambda b, t: (b * n_t + t, 0))
    full = lambda a: pl.BlockSpec(a.shape, lambda b, t: (0,) * a.ndim)
    consts = [g.reshape(1, D_MODEL), wkv_bf, jnp.tile(k_norm, B_HEADS).reshape(1, D_MODEL),
              wf_bf, b_fgate.reshape(1, B_HEADS), ind, ind_t, place_k, ones_k]
    small = jax.ShapeDtypeStruct((m, B_HEADS), F32)
    if prompt:
        cols = pl.BlockSpec((1, bw, tm), lambda b, t: (b, 0, t))
        kv_out = jax.ShapeDtypeStruct((bsz, bw, seq), F32)
        piece = jax.ShapeDtypeStruct((m, B_HEADS), BF16)
        out_specs = [cols, cols, hrow, hrow, hrow, hrow,
                     pl.BlockSpec((1, B_HEADS, tm, LANES), lambda b, t: (b, 0, t, 0)),
                     pl.BlockSpec((1, B_HEADS, VT_ROWS, tm), lambda b, t: (b, 0, 0, t))]
        out_shape = [kv_out, kv_out, small, piece, piece, piece,
                     jax.ShapeDtypeStruct((bsz, B_HEADS, seq, LANES), BF16),
                     jax.ShapeDtypeStruct((bsz, B_HEADS, VT_ROWS, seq), BF16)]
        scratch = [pltpu.VMEM((1, B_HEADS), F32)]
    else:
        out_specs = [row, row, hrow]
        out_shape = [jax.ShapeDtypeStruct((m, D_MODEL), F32)] * 2 + [small]
        scratch = []
    return pl.pallas_call(
        functools.partial(_shared_kv_kernel, prompt),
        grid=(bsz, n_t),
        in_specs=[row] + [full(a) for a in consts],
        out_specs=out_specs,
        out_shape=out_shape,
        scratch_shapes=scratch,
        compiler_params=_cparams(("parallel", "arbitrary")),
        name="shared_kv",
    )(x, *consts)


def _fox_qg(x_ref, g_ref, w_ref, qn_ref, ind_ref, indt_ref):
    xn = _rms_rows(x_ref[...], g_ref[...]).astype(BF16)
    bw = B_HEADS * B_DH
    q = _head_rms(_dot(xn, w_ref[:, 0:bw]), ind_ref[...], indt_ref[...], qn_ref[...])
    pg = _dot(xn, w_ref[:, bw:])
    return q, pg * _sigmoid(pg)


def _fox_proj_kernel(x_ref, g_ref, w_ref, qn_ref, ind_ref, indt_ref, q_ref, sg_ref):
    q, sg = _fox_qg(x_ref, g_ref, w_ref, qn_ref, ind_ref, indt_ref)
    q_ref[...] = q * ATTN_SCALE
    sg_ref[...] = sg


def _fox_proj_aug_kernel(x_ref, g_ref, w_ref, qn_ref, ind_ref, indt_ref, chi_ref, cmid_ref, clo_ref,
                         place_ref, ones_ref, qaug_ref, sg_ref):
    q, sg = _fox_qg(x_ref, g_ref, w_ref, qn_ref, ind_ref, indt_ref)
    sg_ref[...] = sg
    pieces = (chi_ref[...], cmid_ref[...], clo_ref[...])
    bias = _bias_columns(pieces, place_ref, ones_ref)
    _store_heads(q * (ATTN_SCALE * LOG2E), lambda h, lane: bias, qaug_ref)


def _fox_proj(x, g, w_bf, q_norm, tm):
    m = x.shape[0]
    ind, ind_t = _head_indicator()
    row = pl.BlockSpec((tm, D_MODEL), lambda i: (i, 0))
    full = lambda a: pl.BlockSpec(a.shape, lambda i: (0,) * a.ndim)
    consts = [g.reshape(1, D_MODEL), w_bf, jnp.tile(q_norm, B_HEADS).reshape(1, D_MODEL), ind, ind_t]
    out = jax.ShapeDtypeStruct((m, D_MODEL), F32)
    return pl.pallas_call(
        _fox_proj_kernel,
        grid=(m // tm,),
        in_specs=[row] + [full(a) for a in consts],
        out_specs=[row, row],
        out_shape=[out, out],
        compiler_params=_cparams(("parallel",)),
        name="fox_proj",
    )(x, *consts)


def _fox_proj_aug(x, g, w_bf, q_norm, c_pieces, bsz, seq, tm):
    m = x.shape[0]
    n_t = seq // tm
    ind, ind_t = _head_indicator()
    place_q, ones_q, _, _ = _aug_constants()
    row = pl.BlockSpec((tm, D_MODEL), lambda b, t: (b * n_t + t, 0))
    hrow = pl.BlockSpec((tm, B_HEADS), lambda b, t: (b * n_t + t, 0))
    heads = pl.BlockSpec((1, B_HEADS, tm, LANES), lambda b, t: (b, 0, t, 0))
    full = lambda a: pl.BlockSpec(a.shape, lambda b, t: (0,) * a.ndim)
    consts = [g.reshape(1, D_MODEL), w_bf, jnp.tile(q_norm, B_HEADS).reshape(1, D_MODEL), ind, ind_t]
    return pl.pallas_call(
        _fox_proj_aug_kernel,
        grid=(bsz, n_t),
        in_specs=[row] + [full(a) for a in consts] + [hrow] * 3 + [full(place_q), full(ones_q)],
        out_specs=[heads, row],
        out_shape=[jax.ShapeDtypeStruct((bsz, B_HEADS, seq, LANES), BF16),
                   jax.ShapeDtypeStruct((m, D_MODEL), F32)],
        compiler_params=_cparams(("parallel", "parallel")),
        name="fox_proj_aug",
    )(x, *consts, *c_pieces, place_q, ones_q)


def _flash_kernel(tq, q_ref, k_ref, vt_ref, o_ref, s_ref, m_ref, acc_ref):
    qi = pl.program_id(2)

    def block(ki):
        return pl.ds(pl.multiple_of(ki * tq, tq), tq)

    def put_scores(slot, ki):
        for h in range(2):
            s_ref[slot, h] = lax.dot_general(k_ref[0, h, block(ki), :], q_ref[0, h], _NT,
                                             preferred_element_type=F32)

    def consume(slot, ki, diagonal):
        for h in range(2):
            st = s_ref[slot, h]
            if diagonal:
                keep = (lax.broadcasted_iota(jnp.int32, (tq, tq), 0)
                        <= lax.broadcasted_iota(jnp.int32, (tq, tq), 1))
                st = jnp.where(keep, st, -jnp.inf)
            m = m_ref[h]
            m_new = jnp.maximum(m, jnp.max(st, axis=0, keepdims=True))
            p = jnp.exp2(st - m_new).astype(BF16)
            acc_ref[h] = acc_ref[h] * jnp.exp2(m - m_new) + _dot(vt_ref[0, h, :, block(ki)], p)
            m_ref[h] = m_new

    m_ref[...] = jnp.full(m_ref.shape, -jnp.inf, F32)
    acc_ref[...] = jnp.zeros(acc_ref.shape, F32)
    put_scores(0, 0)

    def body(j, carry):
        ki = 2 * j
        put_scores(1, ki + 1)
        consume(0, ki, False)
        put_scores(0, ki + 2)
        consume(1, ki + 1, False)
        return carry

    lax.fori_loop(0, qi // 2, body, 0)

    @pl.when(qi % 2 == 0)
    def _():
        consume(0, qi, True)

    @pl.when(qi % 2 == 1)
    def _():
        put_scores(1, qi)
        consume(0, qi - 1, False)
        consume(1, qi, True)

    outs = [acc_ref[h, 0:B_DH, :] / acc_ref[h, B_DH:B_DH + 1, :] for h in range(2)]
    o_ref[...] = jnp.concatenate(outs, axis=0).T


def _flash(q_aug, k_aug, vt_aug, tq):
    bsz, _, seq, _ = q_aug.shape
    n_q = seq // tq
    qspec = pl.BlockSpec((1, 2, tq, LANES), lambda b, hp, i: (b, hp, i, 0))
    kspec = pl.BlockSpec((1, 2, seq, LANES), lambda b, hp, i: (b, hp, 0, 0))
    vspec = pl.BlockSpec((1, 2, VT_ROWS, seq), lambda b, hp, i: (b, hp, 0, 0))
    return pl.pallas_call(
        functools.partial(_flash_kernel, tq),
        grid=(bsz, B_HEADS // 2, n_q),
        in_specs=[qspec, kspec, vspec],
        out_specs=pl.BlockSpec((tq, LANES), lambda b, hp, i: (b * n_q + i, hp)),
        out_shape=jax.ShapeDtypeStruct((bsz * seq, B_HEADS * B_DH), F32),
        scratch_shapes=[pltpu.VMEM((2, 2, tq, tq), F32), pltpu.VMEM((2, 1, tq), F32),
                        pltpu.VMEM((2, VT_ROWS, tq), F32)],
        compiler_params=_cparams(("parallel", "parallel", "arbitrary")),
        name="fox_flash",
    )(q_aug, k_aug, vt_aug)


def _decode_kernel(npp, pt_ref, qcol_ref, knew_ref, vnew_ref, lfnew_ref, indt_ref, *rest):
    kt_refs = rest[0:npp]
    vt_refs = rest[npp:2 * npp]
    lf_refs = rest[2 * npp:3 * npp]
    o_ref = rest[3 * npp]
    m_ref, l_ref, acc_ref, carry_ref = rest[3 * npp + 1:]
    j = pl.program_id(1)
    page = LANES
    later = (lax.broadcasted_iota(jnp.int32, (page, page), 0)
             > lax.broadcasted_iota(jnp.int32, (page, page), 1)).astype(BF16)

    def update(pages, only_first_lane):
        n = len(pages)
        lf_all = jnp.concatenate([lf for _, _, lf in pages], axis=0)
        sfx_all = _dot3(lf_all, later)
        carry = carry_ref[...]
        logits = []
        for i, (kt, _, lf) in enumerate(pages):
            rows = [jnp.sum(kt[h] * qcol_ref[0, h], axis=0, keepdims=True) for h in range(B_HEADS)]
            lg = jnp.concatenate(rows, axis=0) + sfx_all[i * B_HEADS:(i + 1) * B_HEADS] + carry
            if only_first_lane:
                lane = lax.broadcasted_iota(jnp.int32, (B_HEADS, page), 1)
                lg = jnp.where(lane == 0, lg, -jnp.inf)
            logits.append(lg)
            carry = carry + jnp.sum(lf, axis=-1, keepdims=True)
        carry_ref[...] = carry
        m_old = m_ref[...]
        m_new = m_old
        for lg in logits:
            m_new = jnp.maximum(m_new, jnp.max(lg, axis=-1, keepdims=True))
        alpha = jnp.exp(m_old - m_new)
        probs = [jnp.exp(lg - m_new) for lg in logits]
        l_new = l_ref[...] * alpha
        for p in probs:
            l_new = l_new + p
        l_ref[...] = l_new
        m_ref[...] = m_new
        for h in range(B_HEADS):
            a = acc_ref[h] * alpha[h:h + 1, :]
            for i in range(n):
                a = a + probs[i][h:h + 1, :] * pages[i][1][h]
            acc_ref[h] = a

    @pl.when(j == 0)
    def _():
        m_ref[...] = jnp.full(m_ref.shape, -jnp.inf, F32)
        l_ref[...] = jnp.zeros(l_ref.shape, F32)
        acc_ref[...] = jnp.zeros(acc_ref.shape, F32)
        carry_ref[...] = jnp.zeros(carry_ref.shape, F32)
        update([(knew_ref.at[0], vnew_ref.at[0], lfnew_ref[0])], True)

    update([(kt_refs[i].at[0], vt_refs[i].at[0], lf_refs[i][0]) for i in range(npp)], False)

    @pl.when(j == pl.num_programs(1) - 1)
    def _():
        ones = jnp.ones((SUBLANES, page), BF16)
        lane_sum = lambda x: sum(lax.dot_general(ones, piece, _NT, preferred_element_type=F32)
                                 for piece in _split3(x))[0:1, :]
        num = lane_sum(acc_ref[...].reshape(B_HEADS * B_DH, page))
        den = lane_sum(l_ref[...])
        o_ref[0] = num / _dot3(den, indt_ref[...])


def _positions_on_lanes(a, nb, page):
    a = a.reshape(nb, B_HEADS, -1, 1)
    return jnp.pad(a, ((0, 0), (0, 0), (0, 0), (0, page - 1)))


def _fox_attention_sample(qs, k_new, v_new, lf_new, cache_kt, cache_vt, cache_lft, page_table, npp):
    nb, n_pages = page_table.shape
    page = cache_kt.shape[-1]
    bw = B_HEADS * B_DH
    _, ind_t = _head_indicator()
    n_steps = n_pages // npp
    qcol = jnp.broadcast_to(qs.reshape(nb, B_HEADS, B_DH, 1), (nb, B_HEADS, B_DH, page))
    knew = _positions_on_lanes(k_new, nb, page)
    vnew = _positions_on_lanes(v_new, nb, page)
    lfnew = _positions_on_lanes(lf_new, nb, page).reshape(nb, B_HEADS, page)

    def page_map(i, ndim):
        def index_map(b, j, pt):
            return (pt[b * n_pages + (n_pages - 1 - (j * npp + i))],) + (0,) * (ndim - 1)
        return index_map

    tile = (1, B_HEADS, B_DH, page)
    per_seq_tile = pl.BlockSpec(tile, lambda b, j, pt: (b, 0, 0, 0))
    in_specs = [per_seq_tile, per_seq_tile, per_seq_tile,
                pl.BlockSpec((1, B_HEADS, page), lambda b, j, pt: (b, 0, 0)),
                pl.BlockSpec(ind_t.shape, lambda b, j, pt: (0, 0))]
    in_specs += [pl.BlockSpec(tile, page_map(i, 4)) for i in range(npp)]
    in_specs += [pl.BlockSpec(tile, page_map(i, 4)) for i in range(npp)]
    in_specs += [pl.BlockSpec((1, B_HEADS, page), page_map(i, 3)) for i in range(npp)]
    grid_spec = pltpu.PrefetchScalarGridSpec(
        num_scalar_prefetch=1,
        grid=(nb, n_steps),
        in_specs=in_specs,
        out_specs=pl.BlockSpec((1, 1, bw), lambda b, j, pt: (b, 0, 0)),
        scratch_shapes=[pltpu.VMEM((B_HEADS, 1), F32), pltpu.VMEM((B_HEADS, page), F32),
                        pltpu.VMEM((B_HEADS, B_DH, page), F32), pltpu.VMEM((B_HEADS, 1), F32)],
    )
    o = pl.pallas_call(
        functools.partial(_decode_kernel, npp),
        grid_spec=grid_spec,
        out_shape=jax.ShapeDtypeStruct((nb, 1, bw), F32),
        compiler_params=_cparams(("parallel", "arbitrary")),
        name="fox_decode",
    )(page_table.reshape(-1), qcol, knew, vnew, lfnew, ind_t,
      *([cache_kt] * npp), *([cache_vt] * npp), *([cache_lft] * npp))
    return o.reshape(nb, bw)


def _pick(n, pref):
    return pref if n % pref == 0 else n


def kernel(x_prompt, x_sample, state_hgrn, cache_k, cache_v, cache_logf, page_table,
           norm_in, w_in_a, lb_logits, g_norm_a, w_o_a, norm_kv, w_kv, k_norm, w_fgate, b_fgate,
           w_in_b, q_norm_b, w_o_b):
    bsz, seq, _ = x_prompt.shape
    nb = x_sample.shape[0]
    n_a = w_in_a.shape[0]
    n_b = w_in_b.shape[0]
    tm = _pick(seq, 512)
    tb = _pick(seq, 512)
    tq = _pick(seq, 512)
    npp = _pick(page_table.shape[1], 16)

    hp = x_prompt.reshape(bsz * seq, D_MODEL)
    hs = x_sample.reshape(nb, D_MODEL)
    zero_state = jnp.zeros((bsz, A_HEADS, A_DK, A_DV), F32)
    sp, ss = [], []
    for l in range(n_a):
        w_in = w_in_a[l].astype(BF16)
        w_o = w_o_a[l].astype(BF16)
        q, k, lf, v, sg = _hgrn_proj(hp, norm_in[l], w_in, lb_logits, l, tm)
        o, s_new = _hgrn_rec(q, k, lf, v, zero_state, g_norm_a[l], bsz, seq, tb, REC_HEADS)
        hp = _out_proj(o, sg, hp, w_o, tm)
        sp.append(s_new)
        q, k, lf, v, sg = _hgrn_proj(hs, norm_in[l], w_in, lb_logits, l, nb)
        o, s_new = _hgrn_step(q, k, lf, v, state_hgrn[l], g_norm_a[l])
        hs = _out_proj(o, sg, hs, w_o, nb)
        ss.append(s_new)

    wkv = w_kv.astype(BF16)
    wf = w_fgate.astype(BF16)
    kpt, vpt, lfp, *c_pieces, k_aug, vt_aug = _shared_kv(hp, norm_kv, wkv, k_norm, wf, b_fgate,
                                                        bsz, seq, tm, True)
    heads_last = lambda a: jnp.transpose(a.reshape(bsz, B_HEADS, B_DH, seq), (0, 3, 1, 2))
    ks, vs, lfs = _shared_kv(hs, norm_kv, wkv, k_norm, wf, b_fgate, 1, nb, nb, False)

    ck = jnp.transpose(cache_k, (0, 2, 3, 1))
    cv = jnp.transpose(cache_v, (0, 2, 3, 1))
    clf = jnp.transpose(cache_logf, (0, 2, 1))
    for j in range(n_b):
        w_in = w_in_b[j].astype(BF16)
        w_o = w_o_b[j].astype(BF16)
        g = norm_in[n_a + j]
        q_aug, sg = _fox_proj_aug(hp, g, w_in, q_norm_b[j], c_pieces, bsz, seq, tm)
        o = _flash(q_aug, k_aug, vt_aug, tq)
        hp = _out_proj(o, sg, hp, w_o, tm)
        qs, sg = _fox_proj(hs, g, w_in, q_norm_b[j], nb)
        o = _fox_attention_sample(qs, ks, vs, lfs, ck, cv, clf, page_table, npp)
        hs = _out_proj(o, sg, hs, w_o, nb)

    return (hp.reshape(bsz, seq, D_MODEL), hs.reshape(nb, 1, D_MODEL),
            jnp.stack(sp), jnp.stack(ss),
            heads_last(kpt), ks.reshape(nb, 1, B_HEADS, B_DH),
            heads_last(vpt), vs.reshape(nb, 1, B_HEADS, B_DH),
            lfp.reshape(bsz, seq, B_HEADS), lfs.reshape(nb, 1, B_HEADS))
```

```python
import functools
import math

import jax
import jax.numpy as jnp
import numpy as np
from jax import lax
from jax.experimental import pallas as pl
from jax.experimental.pallas import tpu as pltpu

F32 = jnp.float32
BF16 = jnp.bfloat16

D_MODEL = 1024
A_HEADS = 8
A_DK = 128
A_DV = 128
B_HEADS = 16
B_DH = 64
RMS_EPS = 1e-6
ATTN_SCALE = B_DH ** -0.5
LOG2E = math.log2(math.e)

LANES = 128
SUBLANES = 8
REC_HEADS = 8
FLASH_HEADS = 4
VT_ROWS = 80
REC_CHUNK = 128
VMEM_LIMIT = 48 * 1024 * 1024

_NT = (((1,), (1,)), ((), ()))
_TN = (((0,), (0,)), ((), ()))


def _cparams(sem):
    return pltpu.CompilerParams(dimension_semantics=sem, vmem_limit_bytes=VMEM_LIMIT)


def _dot(a, b):
    return jnp.dot(a, b, preferred_element_type=F32)


def _rms_rows(x, g):
    ms = jnp.mean(x * x, axis=-1, keepdims=True)
    return x * lax.rsqrt(ms + RMS_EPS) * g


def _split3(x):
    hi = x.astype(BF16)
    r1 = x - hi.astype(F32)
    mid = r1.astype(BF16)
    lo = (r1 - mid.astype(F32)).astype(BF16)
    return hi, mid, lo


def _dot3(x, sel):
    hi, mid, lo = _split3(x)
    return _dot(hi, sel) + _dot(mid, sel) + _dot(lo, sel)


def _sel_dot3(sel, x):
    hi, mid, lo = _split3(x)
    return _dot(sel, hi) + _dot(sel, mid) + _dot(sel, lo)


def _sigmoid(x):
    return 1.0 / (1.0 + jnp.exp(-x))


def _dot2(x, sel):
    hi = x.astype(BF16)
    lo = (x - hi.astype(F32)).astype(BF16)
    return _dot(hi, sel) + _dot(lo, sel)


def _head_rms(x, ind, ind_t, gain):
    ssum = _dot2(x * x, ind)
    r = lax.rsqrt(ssum * (1.0 / B_DH) + RMS_EPS)
    return x * _dot2(r, ind_t) * gain


def _hgrn_proj_kernel(layer, x_ref, g_ref, w_ref, lbl_ref, q_ref, k_ref, lf_ref, v_ref, sg_ref):
    xn = _rms_rows(x_ref[...], g_ref[...]).astype(BF16)
    hk = A_HEADS * A_DK

    pq = _dot(xn, w_ref[:, 0:hk])
    q_ref[...] = pq * _sigmoid(pq)

    fz = _dot(xn, w_ref[:, hk:2 * hk])
    ez = jnp.exp(-jnp.abs(fz))
    log_sig = jnp.minimum(fz, 0.0) - jnp.log(1.0 + ez)
    sig_neg = jnp.where(fz >= 0.0, ez, 1.0) / (1.0 + ez)
    if layer == 0:
        lf_ref[...] = log_sig
        k_ref[...] = sig_neg
    else:
        lbl = lbl_ref[...]
        e = jnp.exp(lbl - jnp.max(lbl, axis=0, keepdims=True))
        p = e / jnp.sum(e, axis=0, keepdims=True)
        csum = p[0:1, :]
        for i in range(1, layer + 1):
            csum = csum + p[i:i + 1, :]
        lb = csum - p[0:1, :]
        log_lb = jnp.log(lb)
        b = jnp.log1p(-lb) + log_sig
        lf_ref[...] = jnp.maximum(log_lb, b) + jnp.log(1.0 + jnp.exp(-jnp.abs(log_lb - b)))
        k_ref[...] = (1.0 - lb) * sig_neg

    v_ref[...] = _dot(xn, w_ref[:, 2 * hk:2 * hk + A_HEADS * A_DV])
    pg = _dot(xn, w_ref[:, 2 * hk + A_HEADS * A_DV:])
    sg_ref[...] = pg * _sigmoid(pg)


def _hgrn_proj(x, g, w_bf, lb_logits, layer, tm):
    m = x.shape[0]
    n_in = w_bf.shape[1]
    row = pl.BlockSpec((tm, D_MODEL), lambda i: (i, 0))
    full = lambda shape: pl.BlockSpec(shape, lambda i: (0, 0), pipeline_mode=pl.Buffered(1))
    out = jax.ShapeDtypeStruct((m, D_MODEL), F32)
    return pl.pallas_call(
        functools.partial(_hgrn_proj_kernel, layer),
        grid=(m // tm,),
        in_specs=[row, full((1, D_MODEL)), full((D_MODEL, n_in)), full(lb_logits.shape)],
        out_specs=[row] * 5,
        out_shape=[out] * 5,
        compiler_params=_cparams(("parallel",)),
        name="hgrn_proj",
    )(x, g.reshape(1, D_MODEL), w_bf, lb_logits)


def _rec_chunk(q, k, v, lf, s_t):
    c = q.shape[0]
    nt = c // SUBLANES
    tiles = lambda x: [x[SUBLANES * i:SUBLANES * (i + 1), :] for i in range(nt)]
    cat = lambda ts: jnp.concatenate(ts, axis=0).astype(BF16)
    nt_dot = lambda a, b: lax.dot_general(a, b, _NT, preferred_element_type=F32)
    xor = lax.broadcasted_iota(jnp.int32, (c, c), 0) ^ lax.broadcasted_iota(jnp.int32, (c, c), 1)
    att = jnp.where(xor == 0, nt_dot(q.astype(BF16), k.astype(BF16)), 0.0)
    q_t, k_t = tiles(q), tiles(k)
    pre = tiles(lf)
    tot = tiles(lf)

    row = lax.broadcasted_iota(jnp.int32, (SUBLANES, LANES), 0)
    m = 1
    while m < SUBLANES:
        right = (row & m) != 0
        qt, kt = [], []
        for i in range(nt):
            e = jnp.exp(jnp.where(right, pre[i], tot[i] - pre[i]))
            qt.append(jnp.where(right, q_t[i] * e, 0.0))
            kt.append(jnp.where(right, 0.0, k_t[i] * e))
            down = pltpu.roll(tot[i], m, 0)
            up = pltpu.roll(tot[i], SUBLANES - m, 0)
            pre[i] = pre[i] + jnp.where(right, down, 0.0)
            tot[i] = tot[i] + jnp.where(right, down, up)
        att = att + jnp.where(xor < 2 * m, nt_dot(cat(qt), cat(kt)), 0.0)
        m *= 2

    zero = jnp.zeros((SUBLANES, LANES), F32)
    mv = 1
    while mv < nt:
        qt, kt = [zero] * nt, [zero] * nt
        for base in range(0, nt, 2 * mv):
            for j in range(mv):
                lt, rt = base + j, base + mv + j
                kt[lt] = k_t[lt] * jnp.exp(tot[lt] - pre[lt])
                qt[rt] = q_t[rt] * jnp.exp(pre[rt])
        a = nt_dot(cat(qt), cat(kt))
        att = att + (a if 2 * mv == nt else jnp.where(xor < 2 * mv * SUBLANES, a, 0.0))
        for base in range(0, nt, 2 * mv):
            left_total = tot[base]
            both = left_total + tot[base + mv]
            for j in range(mv):
                pre[base + mv + j] = pre[base + mv + j] + left_total
            for j in range(2 * mv):
                tot[base + j] = both
        mv *= 2

    vb = v.astype(BF16)
    qg = cat([q_t[i] * jnp.exp(pre[i]) for i in range(nt)])
    kh = cat([k_t[i] * jnp.exp(tot[i] - pre[i]) for i in range(nt)])
    o = nt_dot(qg, s_t.astype(BF16)) + _dot(att.astype(BF16), vb)
    s_new = s_t * jnp.exp(tot[0][0:1, :]) + lax.dot_general(vb, kh, _TN, preferred_element_type=F32)
    return o, s_new


def _hgrn_rec_kernel(n_chunks, nh, q_ref, k_ref, lf_ref, v_ref, s0_ref, gn_ref, o_ref, s_out_ref, st_ref):
    t = pl.program_id(2)

    @pl.when(t == 0)
    def _():
        for h in range(nh):
            st_ref[h] = s0_ref[0, h].T

    def body(ci, carry):
        rows = pl.ds(pl.multiple_of(ci * REC_CHUNK, REC_CHUNK), REC_CHUNK)
        for h in range(nh):
            cols = slice(h * A_DK, (h + 1) * A_DK)
            o, s_new = _rec_chunk(q_ref[rows, cols], k_ref[rows, cols], v_ref[rows, cols],
                                  lf_ref[rows, cols], st_ref[h])
            st_ref[h] = s_new
            o_ref[rows, cols] = _rms_rows(o, gn_ref[...])
        return carry

    lax.fori_loop(0, n_chunks, body, 0)

    @pl.when(t == pl.num_programs(2) - 1)
    def _():
        for h in range(nh):
            s_out_ref[0, h] = st_ref[h].T


def _hgrn_rec(q, k, lf, v, s0, g_norm, bsz, seq, tb, nh):
    n_t = seq // tb
    blk = pl.BlockSpec((tb, nh * A_DK), lambda b, h, t: (b * n_t + t, h))
    st = pl.BlockSpec((1, nh, A_DK, A_DV), lambda b, h, t: (b, h, 0, 0))
    return pl.pallas_call(
        functools.partial(_hgrn_rec_kernel, tb // REC_CHUNK, nh),
        grid=(bsz, A_HEADS // nh, n_t),
        in_specs=[blk, blk, blk, blk, st, pl.BlockSpec((1, A_DV), lambda b, h, t: (0, 0))],
        out_specs=[blk, st],
        out_shape=[jax.ShapeDtypeStruct(q.shape, F32), jax.ShapeDtypeStruct(s0.shape, F32)],
        scratch_shapes=[pltpu.VMEM((nh, A_DV, A_DK), F32)],
        compiler_params=_cparams(("parallel", "parallel", "arbitrary")),
        name="hgrn_rec",
    )(q, k, lf, v, s0, g_norm.reshape(1, A_DV))


def _hgrn_step_kernel(nb, qt_ref, kt_ref, lft_ref, v_ref, s_ref, gn_ref, o_ref, s_out_ref):
    qt = qt_ref[0]
    kt = kt_ref[0]
    ft = jnp.exp(lft_ref[0])
    for b in range(nb):
        col = lambda a: jnp.broadcast_to(a[:, b:b + 1], (A_DK, A_DV))
        s_new = col(ft) * s_ref[b, 0] + col(kt) * v_ref[b:b + 1, :]
        s_out_ref[b, 0] = s_new
        o = jnp.sum(col(qt) * s_new, axis=0, keepdims=True)
        o_ref[b:b + 1, :] = _rms_rows(o, gn_ref[...])


def _hgrn_step(q, k, lf, v, s, g_norm):
    nb = q.shape[0]
    tr = lambda a: a.reshape(nb, A_HEADS, A_DK).transpose(1, 2, 0)
    colspec = pl.BlockSpec((1, A_DK, nb), lambda h: (h, 0, 0))
    vspec = pl.BlockSpec((nb, A_DV), lambda h: (0, h))
    sspec = pl.BlockSpec((nb, 1, A_DK, A_DV), lambda h: (0, h, 0, 0))
    return pl.pallas_call(
        functools.partial(_hgrn_step_kernel, nb),
        grid=(A_HEADS,),
        in_specs=[colspec, colspec, colspec, vspec, sspec, pl.BlockSpec((1, A_DV), lambda h: (0, 0))],
        out_specs=[vspec, sspec],
        out_shape=[jax.ShapeDtypeStruct(v.shape, F32), jax.ShapeDtypeStruct(s.shape, F32)],
        compiler_params=_cparams(("parallel",)),
        name="hgrn_step",
    )(tr(q), tr(k), tr(lf), v, s, g_norm.reshape(1, A_DV))


def _out_proj_kernel(a_ref, sg_ref, x_ref, w_ref, y_ref):
    y_ref[...] = x_ref[...] + _dot((a_ref[...] * sg_ref[...]).astype(BF16), w_ref[...])


def _out_proj(a, sg, x, w_bf, tm):
    m = x.shape[0]
    row = pl.BlockSpec((tm, D_MODEL), lambda i: (i, 0))
    return pl.pallas_call(
        _out_proj_kernel,
        grid=(m // tm,),
        in_specs=[row, row, row, pl.BlockSpec(w_bf.shape, lambda i: (0, 0))],
        out_specs=row,
        out_shape=jax.ShapeDtypeStruct(x.shape, F32),
        compiler_params=_cparams(("parallel",)),
        name="out_proj",
    )(a, sg, x, w_bf)


BIAS_COLS = 6


def _bias_lane(h):
    return (B_DH if h % 2 == 0 else 0) + BIAS_COLS * (h // 2)


def _aug_constants():
    place_q = np.zeros((3, B_HEADS, LANES), np.float32)
    place_k = np.zeros((3, B_HEADS, LANES), np.float32)
    ones_q = np.zeros((1, LANES), np.float32)
    ones_k = np.zeros((1, LANES), np.float32)
    for h in range(B_HEADS):
        base = _bias_lane(h)
        for p in range(3):
            place_q[p, h, base + p] = 1.0
            ones_q[0, base + 3 + p] = 1.0
            ones_k[0, base + p] = 1.0
            place_k[p, h, base + 3 + p] = -1.0
    as_bf = lambda a: jnp.asarray(a, BF16)
    return as_bf(place_q), jnp.asarray(ones_q), as_bf(place_k), jnp.asarray(ones_k)


def _bias_columns(pieces, place_ref, ones_ref):
    out = ones_ref[...]
    for p in range(3):
        out = out + _dot(pieces[p], place_ref[p])
    return out


def _store_heads(x, other, o_ref):
    lane = lax.broadcasted_iota(jnp.int32, (x.shape[0], LANES), 1)
    for h in range(B_HEADS):
        own = (lane < B_DH) if h % 2 == 0 else (lane >= B_DH)
        xv = x[:, (h // 2) * LANES:(h // 2 + 1) * LANES]
        o_ref[0, h] = jnp.where(own, xv, other(h, lane)).astype(BF16)


def _shared_kv_kernel(prompt, x_ref, g_ref, wkv_ref, kn_ref, wf_ref, bf_ref, ind_ref, indt_ref,
                      place_ref, ones_ref, k_ref, v_ref, lf_ref, *rest):
    tm = x_ref.shape[0]
    u = _rms_rows(x_ref[...], g_ref[...]).astype(BF16)
    bw = B_HEADS * B_DH
    k = _head_rms(_dot(u, wkv_ref[:, 0:bw]), ind_ref[...], indt_ref[...], kn_ref[...])
    v = _dot(u, wkv_ref[:, bw:])
    z = _dot(u, wf_ref[...]) + bf_ref[...]
    lf = jnp.minimum(z, 0.0) - jnp.log(1.0 + jnp.exp(-jnp.abs(z)))
    lf_ref[...] = lf
    if not prompt:
        k_ref[...] = k
        v_ref[...] = v
        return

    chi_ref, cmid_ref, clo_ref, kaug_ref, vtaug_ref, carry_ref = rest
    tail = jnp.where(lax.broadcasted_iota(jnp.int32, (VT_ROWS - B_DH, tm), 0) == 0, 1.0, 0.0).astype(BF16)
    for j in range(B_HEADS // 2):
        cols = slice(j * LANES, (j + 1) * LANES)
        k_ref[0, cols, :] = k[:, cols].T
        v_pair = v[:, cols].T
        v_ref[0, cols, :] = v_pair
        for half in range(2):
            vtaug_ref[0, 2 * j + half, 0:B_DH, :] = v_pair[half * B_DH:(half + 1) * B_DH, :].astype(BF16)
            vtaug_ref[0, 2 * j + half, B_DH:VT_ROWS, :] = tail

    @pl.when(pl.program_id(1) == 0)
    def _():
        carry_ref[...] = jnp.zeros_like(carry_ref)

    lower = (lax.broadcasted_iota(jnp.int32, (tm, tm), 0)
             >= lax.broadcasted_iota(jnp.int32, (tm, tm), 1)).astype(BF16)
    c = _sel_dot3(lower, lf) + carry_ref[...]
    carry_ref[...] = c[tm - 1:tm, :]
    pieces = _split3(c * LOG2E)
    chi_ref[...], cmid_ref[...], clo_ref[...] = pieces
    bias = _bias_columns(pieces, place_ref, ones_ref)

    lane_row = lax.broadcasted_iota(jnp.int32, (1, LANES), 1)

    def own_bias(h, lane):
        offset = lane_row - _bias_lane(h)
        return bias * jnp.where((offset >= 0) & (offset < BIAS_COLS), 1.0, 0.0)

    _store_heads(k, own_bias, kaug_ref)


def _head_indicator():
    ind = (jnp.arange(B_HEADS * B_DH)[:, None] // B_DH == jnp.arange(B_HEADS)[None, :])
    return ind.astype(BF16), ind.T.astype(BF16)


def _shared_kv(x, g, wkv_bf, k_norm, wf_bf, b_fgate, bsz, seq, tm, prompt):
    m = x.shape[0]
    n_t = seq // tm
    bw = B_HEADS * B_DH
    ind, ind_t = _head_indicator()
    _, _, place_k, ones_k = _aug_constants()
    row = pl.BlockSpec((tm, D_MODEL), lambda b, t: (b * n_t + t, 0))
    hrow = pl.BlockSpec((tm, B_HEADS), lambda b, t: (b * n_t + t, 0))
    full = lambda a: pl.BlockSpec(a.shape, lambda b, t: (0,) * a.ndim)
    consts = [g.reshape(1, D_MODEL), wkv_bf, jnp.tile(k_norm, B_HEADS).reshape(1, D_MODEL),
              wf_bf, b_fgate.reshape(1, B_HEADS), ind, ind_t, place_k, ones_k]
    small = jax.ShapeDtypeStruct((m, B_HEADS), F32)
    if prompt:
        cols = pl.BlockSpec((1, bw, tm), lambda b, t: (b, 0, t))
        kv_out = jax.ShapeDtypeStruct((bsz, bw, seq), F32)
        piece = jax.ShapeDtypeStruct((m, B_HEADS), BF16)
        out_specs = [cols, cols, hrow, hrow, hrow, hrow,
                     pl.BlockSpec((1, B_HEADS, tm, LANES), lambda b, t: (b, 0, t, 0)),
                     pl.BlockSpec((1, B_HEADS, VT_ROWS, tm), lambda b, t: (b, 0, 0, t))]
        out_shape = [kv_out, kv_out, small, piece, piece, piece,
                     jax.ShapeDtypeStruct((bsz, B_HEADS, seq, LANES), BF16),
                     jax.ShapeDtypeStruct((bsz, B_HEADS, VT_ROWS, seq), BF16)]
        scratch = [pltpu.VMEM((1, B_HEADS), F32)]
    else:
        out_specs = [row, row, hrow]
        out_shape = [jax.ShapeDtypeStruct((m, D_MODEL), F32)] * 2 + [small]
        scratch = []
    return pl.pallas_call(
        functools.partial(_shared_kv_kernel, prompt),
        grid=(bsz, n_t),
        in_specs=[row] + [full(a) for a in consts],
        out_specs=out_specs,
        out_shape=out_shape,
        scratch_shapes=scratch,
        compiler_params=_cparams(("parallel", "arbitrary")),
        name="shared_kv",
    )(x, *consts)


def _fox_qg(x_ref, g_ref, w_ref, qn_ref, ind_ref, indt_ref):
    xn = _rms_rows(x_ref[...], g_ref[...]).astype(BF16)
    bw = B_HEADS * B_DH
    q = _head_rms(_dot(xn, w_ref[:, 0:bw]), ind_ref[...], indt_ref[...], qn_ref[...])
    pg = _dot(xn, w_ref[:, bw:])
    return q, pg * _sigmoid(pg)


def _fox_proj_kernel(x_ref, g_ref, w_ref, qn_ref, ind_ref, indt_ref, q_ref, sg_ref):
    q, sg = _fox_qg(x_ref, g_ref, w_ref, qn_ref, ind_ref, indt_ref)
    q_ref[...] = q * ATTN_SCALE
    sg_ref[...] = sg


def _fox_proj_aug_kernel(x_ref, g_ref, w_ref, qn_ref, ind_ref, indt_ref, chi_ref, cmid_ref, clo_ref,
                         place_ref, ones_ref, qaug_ref, sg_ref):
    q, sg = _fox_qg(x_ref, g_ref, w_ref, qn_ref, ind_ref, indt_ref)
    sg_ref[...] = sg
    pieces = (chi_ref[...], cmid_ref[...], clo_ref[...])
    bias = _bias_columns(pieces, place_ref, ones_ref)
    _store_heads(q * (ATTN_SCALE * LOG2E), lambda h, lane: bias, qaug_ref)


def _fox_proj(x, g, w_bf, q_norm, tm):
    m = x.shape[0]
    ind, ind_t = _head_indicator()
    row = pl.BlockSpec((tm, D_MODEL), lambda i: (i, 0))
    full = lambda a: pl.BlockSpec(a.shape, lambda i: (0,) * a.ndim)
    consts = [g.reshape(1, D_MODEL), w_bf, jnp.tile(q_norm, B_HEADS).reshape(1, D_MODEL), ind, ind_t]
    out = jax.ShapeDtypeStruct((m, D_MODEL), F32)
    return pl.pallas_call(
        _fox_proj_kernel,
        grid=(m // tm,),
        in_specs=[row] + [full(a) for a in consts],
        out_specs=[row, row],
        out_shape=[out, out],
        compiler_params=_cparams(("parallel",)),
        name="fox_proj",
    )(x, *consts)


def _fox_proj_aug(x, g, w_bf, q_norm, c_pieces, bsz, seq, tm):
    m = x.shape[0]
    n_t = seq // tm
    ind, ind_t = _head_indicator()
    place_q, ones_q, _, _ = _aug_constants()
    row = pl.BlockSpec((tm, D_MODEL), lambda b, t: (b * n_t + t, 0))
    hrow = pl.BlockSpec((tm, B_HEADS), lambda b, t: (b * n_t + t, 0))
    heads = pl.BlockSpec((1, B_HEADS, tm, LANES), lambda b, t: (b, 0, t, 0))
    full = lambda a: pl.BlockSpec(a.shape, lambda b, t: (0,) * a.ndim)
    consts = [g.reshape(1, D_MODEL), w_bf, jnp.tile(q_norm, B_HEADS).reshape(1, D_MODEL), ind, ind_t]
    return pl.pallas_call(
        _fox_proj_aug_kernel,
        grid=(bsz, n_t),
        in_specs=[row] + [full(a) for a in consts] + [hrow] * 3 + [full(place_q), full(ones_q)],
        out_specs=[heads, row],
        out_shape=[jax.ShapeDtypeStruct((bsz, B_HEADS, seq, LANES), BF16),
                   jax.ShapeDtypeStruct((m, D_MODEL), F32)],
        compiler_params=_cparams(("parallel", "parallel")),
        name="fox_proj_aug",
    )(x, *consts, *c_pieces, place_q, ones_q)


def _flash_kernel(tq, q_ref, k_ref, vt_ref, o_ref, s_ref, m_ref, acc_ref):
    qi = pl.program_id(2)

    def block(ki):
        return pl.ds(pl.multiple_of(ki * tq, tq), tq)

    def put_scores(slot, ki):
        for h in range(FLASH_HEADS):
            s_ref[slot, h] = lax.dot_general(k_ref[0, h, block(ki), :], q_ref[0, h], _NT,
                                             preferred_element_type=F32)

    def consume(slot, ki, diagonal):
        for h in range(FLASH_HEADS):
            st = s_ref[slot, h]
            if diagonal:
                keep = (lax.broadcasted_iota(jnp.int32, (tq, tq), 0)
                        <= lax.broadcasted_iota(jnp.int32, (tq, tq), 1))
                st = jnp.where(keep, st, -jnp.inf)
            m = m_ref[h]
            m_new = jnp.maximum(m, jnp.max(st, axis=0, keepdims=True))
            p = jnp.exp2(st - m_new).astype(BF16)
            acc_ref[h] = acc_ref[h] * jnp.exp2(m - m_new) + _dot(vt_ref[0, h, :, block(ki)], p)
            m_ref[h] = m_new

    m_ref[...] = jnp.full(m_ref.shape, -jnp.inf, F32)
    acc_ref[...] = jnp.zeros(acc_ref.shape, F32)
    put_scores(0, 0)

    def pair(ki):
        put_scores(1, ki + 1)
        consume(0, ki, False)
        put_scores(0, ki + 2)
        consume(1, ki + 1, False)

    def body(j, carry):
        pair(4 * j)
        pair(4 * j + 2)
        return carry

    lax.fori_loop(0, qi // 4, body, 0)

    @pl.when(qi % 4 >= 2)
    def _():
        pair(4 * (qi // 4))

    @pl.when(qi % 2 == 0)
    def _():
        consume(0, qi, True)

    @pl.when(qi % 2 == 1)
    def _():
        put_scores(1, qi)
        consume(0, qi - 1, False)
        consume(1, qi, True)

    outs = [acc_ref[h, 0:B_DH, :] / acc_ref[h, B_DH:B_DH + 1, :] for h in range(FLASH_HEADS)]
    o_ref[...] = jnp.concatenate(outs, axis=0).T


def _flash(q_aug, k_aug, vt_aug, tq):
    bsz, _, seq, _ = q_aug.shape
    n_q = seq // tq
    qspec = pl.BlockSpec((1, FLASH_HEADS, tq, LANES), lambda b, hp, i: (b, hp, i, 0))
    kspec = pl.BlockSpec((1, FLASH_HEADS, seq, LANES), lambda b, hp, i: (b, hp, 0, 0))
    vspec = pl.BlockSpec((1, FLASH_HEADS, VT_ROWS, seq), lambda b, hp, i: (b, hp, 0, 0))
    return pl.pallas_call(
        functools.partial(_flash_kernel, tq),
        grid=(bsz, B_HEADS // FLASH_HEADS, n_q),
        in_specs=[qspec, kspec, vspec],
        out_specs=pl.BlockSpec((tq, FLASH_HEADS * B_DH), lambda b, hp, i: (b * n_q + i, hp)),
        out_shape=jax.ShapeDtypeStruct((bsz * seq, B_HEADS * B_DH), F32),
        scratch_shapes=[pltpu.VMEM((2, FLASH_HEADS, tq, tq), F32), pltpu.VMEM((FLASH_HEADS, 1, tq), F32),
                        pltpu.VMEM((FLASH_HEADS, VT_ROWS, tq), F32)],
        compiler_params=_cparams(("parallel", "parallel", "arbitrary")),
        name="fox_flash",
    )(q_aug, k_aug, vt_aug)


def _decode_kernel(npp, pt_ref, qcol_ref, knew_ref, vnew_ref, lfnew_ref, indt_ref, *rest):
    kt_refs = rest[0:npp]
    vt_refs = rest[npp:2 * npp]
    lf_refs = rest[2 * npp:3 * npp]
    o_ref = rest[3 * npp]
    m_ref, l_ref, acc_ref, carry_ref = rest[3 * npp + 1:]
    j = pl.program_id(1)
    page = LANES
    later = (lax.broadcasted_iota(jnp.int32, (page, page), 0)
             > lax.broadcasted_iota(jnp.int32, (page, page), 1)).astype(BF16)

    def update(pages, only_first_lane):
        n = len(pages)
        lf_all = jnp.concatenate([lf for _, _, lf in pages], axis=0)
        sfx_all = _dot3(lf_all, later)
        carry = carry_ref[...]
        logits = []
        for i, (kt, _, lf) in enumerate(pages):
            rows = [jnp.sum(kt[h] * qcol_ref[0, h], axis=0, keepdims=True) for h in range(B_HEADS)]
            lg = jnp.concatenate(rows, axis=0) + sfx_all[i * B_HEADS:(i + 1) * B_HEADS] + carry
            if only_first_lane:
                lane = lax.broadcasted_iota(jnp.int32, (B_HEADS, page), 1)
                lg = jnp.where(lane == 0, lg, -jnp.inf)
            logits.append(lg)
            carry = carry + jnp.sum(lf, axis=-1, keepdims=True)
        carry_ref[...] = carry
        m_old = m_ref[...]
        m_new = m_old
        for lg in logits:
            m_new = jnp.maximum(m_new, jnp.max(lg, axis=-1, keepdims=True))
        alpha = jnp.exp(m_old - m_new)
        probs = [jnp.exp(lg - m_new) for lg in logits]
        l_new = l_ref[...] * alpha
        for p in probs:
            l_new = l_new + p
        l_ref[...] = l_new
        m_ref[...] = m_new
        for h in range(B_HEADS):
            a = acc_ref[h] * alpha[h:h + 1, :]
            for i in range(n):
                a = a + probs[i][h:h + 1, :] * pages[i][1][h]
            acc_ref[h] = a

    @pl.when(j == 0)
    def _():
        m_ref[...] = jnp.full(m_ref.shape, -jnp.inf, F32)
        l_ref[...] = jnp.zeros(l_ref.shape, F32)
        acc_ref[...] = jnp.zeros(acc_ref.shape, F32)
        carry_ref[...] = jnp.zeros(carry_ref.shape, F32)
        update([(knew_ref.at[0], vnew_ref.at[0], lfnew_ref[0])], True)

    update([(kt_refs[i].at[0], vt_refs[i].at[0], lf_refs[i][0]) for i in range(npp)], False)

    @pl.when(j == pl.num_programs(1) - 1)
    def _():
        ones = jnp.ones((SUBLANES, page), BF16)
        lane_sum = lambda x: sum(lax.dot_general(ones, piece, _NT, preferred_element_type=F32)
                                 for piece in _split3(x))[0:1, :]
        num = lane_sum(acc_ref[...].reshape(B_HEADS * B_DH, page))
        den = lane_sum(l_ref[...])
        o_ref[0] = num / _dot3(den, indt_ref[...])


def _positions_on_lanes(a, nb, page):
    a = a.reshape(nb, B_HEADS, -1, 1)
    return jnp.pad(a, ((0, 0), (0, 0), (0, 0), (0, page - 1)))


def _fox_attention_sample(qs, k_new, v_new, lf_new, cache_kt, cache_vt, cache_lft, page_table, npp):
    nb, n_pages = page_table.shape
    page = cache_kt.shape[-1]
    bw = B_HEADS * B_DH
    _, ind_t = _head_indicator()
    n_steps = n_pages // npp
    qcol = jnp.broadcast_to(qs.reshape(nb, B_HEADS, B_DH, 1), (nb, B_HEADS, B_DH, page))
    knew = _positions_on_lanes(k_new, nb, page)
    vnew = _positions_on_lanes(v_new, nb, page)
    lfnew = _positions_on_lanes(lf_new, nb, page).reshape(nb, B_HEADS, page)

    def page_map(i, ndim):
        def index_map(b, j, pt):
            return (pt[b * n_pages + (n_pages - 1 - (j * npp + i))],) + (0,) * (ndim - 1)
        return index_map

    tile = (1, B_HEADS, B_DH, page)
    per_seq_tile = pl.BlockSpec(tile, lambda b, j, pt: (b, 0, 0, 0))
    in_specs = [per_seq_tile, per_seq_tile, per_seq_tile,
                pl.BlockSpec((1, B_HEADS, page), lambda b, j, pt: (b, 0, 0)),
                pl.BlockSpec(ind_t.shape, lambda b, j, pt: (0, 0))]
    in_specs += [pl.BlockSpec(tile, page_map(i, 4)) for i in range(npp)]
    in_specs += [pl.BlockSpec(tile, page_map(i, 4)) for i in range(npp)]
    in_specs += [pl.BlockSpec((1, B_HEADS, page), page_map(i, 3)) for i in range(npp)]
    grid_spec = pltpu.PrefetchScalarGridSpec(
        num_scalar_prefetch=1,
        grid=(nb, n_steps),
        in_specs=in_specs,
        out_specs=pl.BlockSpec((1, 1, bw), lambda b, j, pt: (b, 0, 0)),
        scratch_shapes=[pltpu.VMEM((B_HEADS, 1), F32), pltpu.VMEM((B_HEADS, page), F32),
                        pltpu.VMEM((B_HEADS, B_DH, page), F32), pltpu.VMEM((B_HEADS, 1), F32)],
    )
    o = pl.pallas_call(
        functools.partial(_decode_kernel, npp),
        grid_spec=grid_spec,
        out_shape=jax.ShapeDtypeStruct((nb, 1, bw), F32),
        compiler_params=_cparams(("parallel", "arbitrary")),
        name="fox_decode",
    )(page_table.reshape(-1), qcol, knew, vnew, lfnew, ind_t,
      *([cache_kt] * npp), *([cache_vt] * npp), *([cache_lft] * npp))
    return o.reshape(nb, bw)


def _pick(n, pref):
    return pref if n % pref == 0 else n


def kernel(x_prompt, x_sample, state_hgrn, cache_k, cache_v, cache_logf, page_table,
           norm_in, w_in_a, lb_logits, g_norm_a, w_o_a, norm_kv, w_kv, k_norm, w_fgate, b_fgate,
           w_in_b, q_norm_b, w_o_b):
    bsz, seq, _ = x_prompt.shape
    nb = x_sample.shape[0]
    n_a = w_in_a.shape[0]
    n_b = w_in_b.shape[0]
    tm = _pick(seq, 512)
    tb = _pick(seq, 512)
    tq = _pick(seq, 512)
    npp = _pick(page_table.shape[1], 16)

    hp = x_prompt.reshape(bsz * seq, D_MODEL)
    hs = x_sample.reshape(nb, D_MODEL)
    zero_state = jnp.zeros((bsz, A_HEADS, A_DK, A_DV), F32)
    sp, ss = [], []
    for l in range(n_a):
        w_in = w_in_a[l].astype(BF16)
        w_o = w_o_a[l].astype(BF16)
        q, k, lf, v, sg = _hgrn_proj(hp, norm_in[l], w_in, lb_logits, l, tm)
        o, s_new = _hgrn_rec(q, k, lf, v, zero_state, g_norm_a[l], bsz, seq, tb, REC_HEADS)
        hp = _out_proj(o, sg, hp, w_o, tm)
        sp.append(s_new)
        q, k, lf, v, sg = _hgrn_proj(hs, norm_in[l], w_in, lb_logits, l, nb)
        o, s_new = _hgrn_step(q, k, lf, v, state_hgrn[l], g_norm_a[l])
        hs = _out_proj(o, sg, hs, w_o, nb)
        ss.append(s_new)

    wkv = w_kv.astype(BF16)
    wf = w_fgate.astype(BF16)
    kpt, vpt, lfp, *c_pieces, k_aug, vt_aug = _shared_kv(hp, norm_kv, wkv, k_norm, wf, b_fgate,
                                                        bsz, seq, tm, True)
    heads_last = lambda a: jnp.transpose(a.reshape(bsz, B_HEADS, B_DH, seq), (0, 3, 1, 2))
    ks, vs, lfs = _shared_kv(hs, norm_kv, wkv, k_norm, wf, b_fgate, 1, nb, nb, False)

    ck = jnp.transpose(cache_k, (0, 2, 3, 1))
    cv = jnp.transpose(cache_v, (0, 2, 3, 1))
    clf = jnp.transpose(cache_logf, (0, 2, 1))
    for j in range(n_b):
        w_in = w_in_b[j].astype(BF16)
        w_o = w_o_b[j].astype(BF16)
        g = norm_in[n_a + j]
        q_aug, sg = _fox_proj_aug(hp, g, w_in, q_norm_b[j], c_pieces, bsz, seq, tm)
        o = _flash(q_aug, k_aug, vt_aug, tq)
        hp = _out_proj(o, sg, hp, w_o, tm)
        qs, sg = _fox_proj(hs, g, w_in, q_norm_b[j], nb)
        o = _fox_attention_sample(qs, ks, vs, lfs, ck, cv, clf, page_table, npp)
        hs = _out_proj(o, sg, hs, w_o, nb)

    return (hp.reshape(bsz, seq, D_MODEL), hs.reshape(nb, 1, D_MODEL),
            jnp.stack(sp), jnp.stack(ss),
            heads_last(kpt), ks.reshape(nb, 1, B_HEADS, B_DH),
            heads_last(vpt), vs.reshape(nb, 1, B_HEADS, B_DH),
            lfp.reshape(bsz, seq, B_HEADS), lfs.reshape(nb, 1, B_HEADS))
```

```python
import functools
import math

import jax
import jax.numpy as jnp
import numpy as np
from jax import lax
from jax.experimental import pallas as pl
from jax.experimental.pallas import tpu as pltpu

F32 = jnp.float32
BF16 = jnp.bfloat16

D_MODEL = 1024
A_HEADS = 8
A_DK = 128
A_DV = 128
B_HEADS = 16
B_DH = 64
RMS_EPS = 1e-6
ATTN_SCALE = B_DH ** -0.5
LOG2E = math.log2(math.e)

LANES = 128
SUBLANES = 8
MXU_WIDTH = 256
REC_HEADS = 8
FLASH_HEADS = 4
VT_ROWS = 80
REC_CHUNK = 128
VMEM_LIMIT = 48 * 1024 * 1024

_NT = (((1,), (1,)), ((), ()))
_TN = (((0,), (0,)), ((), ()))


def _cparams(sem):
    return pltpu.CompilerParams(dimension_semantics=sem, vmem_limit_bytes=VMEM_LIMIT)


def _dot(a, b):
    return jnp.dot(a, b, preferred_element_type=F32)


def _rms_rows(x, g):
    ms = jnp.mean(x * x, axis=-1, keepdims=True)
    return x * lax.rsqrt(ms + RMS_EPS) * g


def _split3(x):
    hi = x.astype(BF16)
    r1 = x - hi.astype(F32)
    mid = r1.astype(BF16)
    lo = (r1 - mid.astype(F32)).astype(BF16)
    return hi, mid, lo


def _dot3(x, sel):
    hi, mid, lo = _split3(x)
    return _dot(hi, sel) + _dot(mid, sel) + _dot(lo, sel)


def _sel_dot3(sel, x):
    hi, mid, lo = _split3(x)
    return _dot(sel, hi) + _dot(sel, mid) + _dot(sel, lo)


def _sigmoid(x):
    return 1.0 / (1.0 + jnp.exp(-x))


def _dot2(x, sel):
    hi = x.astype(BF16)
    lo = (x - hi.astype(F32)).astype(BF16)
    return _dot(hi, sel) + _dot(lo, sel)


def _head_rms(x, ind, ind_t, gain):
    ssum = _dot2(x * x, ind)
    r = lax.rsqrt(ssum * (1.0 / B_DH) + RMS_EPS)
    return x * _dot2(r, ind_t) * gain


def _hgrn_proj_kernel(layer, x_ref, g_ref, w_ref, lbl_ref, q_ref, k_ref, lf_ref, v_ref, sg_ref):
    xn = _rms_rows(x_ref[...], g_ref[...]).astype(BF16)
    hk = A_HEADS * A_DK

    pq = _dot(xn, w_ref[:, 0:hk])
    q_ref[...] = pq * _sigmoid(pq)

    fz = _dot(xn, w_ref[:, hk:2 * hk])
    ez = jnp.exp(-jnp.abs(fz))
    log_sig = jnp.minimum(fz, 0.0) - jnp.log(1.0 + ez)
    sig_neg = jnp.where(fz >= 0.0, ez, 1.0) / (1.0 + ez)
    if layer == 0:
        lf_ref[...] = log_sig
        k_ref[...] = sig_neg
    else:
        lbl = lbl_ref[...]
        e = jnp.exp(lbl - jnp.max(lbl, axis=0, keepdims=True))
        p = e / jnp.sum(e, axis=0, keepdims=True)
        csum = p[0:1, :]
        for i in range(1, layer + 1):
            csum = csum + p[i:i + 1, :]
        lb = csum - p[0:1, :]
        log_lb = jnp.log(lb)
        b = jnp.log1p(-lb) + log_sig
        lf_ref[...] = jnp.maximum(log_lb, b) + jnp.log(1.0 + jnp.exp(-jnp.abs(log_lb - b)))
        k_ref[...] = (1.0 - lb) * sig_neg

    v_ref[...] = _dot(xn, w_ref[:, 2 * hk:2 * hk + A_HEADS * A_DV])
    pg = _dot(xn, w_ref[:, 2 * hk + A_HEADS * A_DV:])
    sg_ref[...] = pg * _sigmoid(pg)


def _hgrn_proj(x, g, w_bf, lb_logits, layer, tm):
    m = x.shape[0]
    n_in = w_bf.shape[1]
    row = pl.BlockSpec((tm, D_MODEL), lambda i: (i, 0))
    full = lambda shape: pl.BlockSpec(shape, lambda i: (0, 0), pipeline_mode=pl.Buffered(1))
    out = jax.ShapeDtypeStruct((m, D_MODEL), F32)
    return pl.pallas_call(
        functools.partial(_hgrn_proj_kernel, layer),
        grid=(m // tm,),
        in_specs=[row, full((1, D_MODEL)), full((D_MODEL, n_in)), full(lb_logits.shape)],
        out_specs=[row] * 5,
        out_shape=[out] * 5,
        compiler_params=_cparams(("parallel",)),
        name="hgrn_proj",
    )(x, g.reshape(1, D_MODEL), w_bf, lb_logits)


def _rec_chunk(q, k, v, lf, s_t):
    c = q.shape[0]
    nt = c // SUBLANES
    tiles = lambda x: [x[SUBLANES * i:SUBLANES * (i + 1), :] for i in range(nt)]
    cat = lambda ts: jnp.concatenate(ts, axis=0).astype(BF16)
    nt_dot = lambda a, b: lax.dot_general(a, b, _NT, preferred_element_type=F32)
    xor = lax.broadcasted_iota(jnp.int32, (c, c), 0) ^ lax.broadcasted_iota(jnp.int32, (c, c), 1)
    att = jnp.where(xor == 0, nt_dot(q.astype(BF16), k.astype(BF16)), 0.0)
    q_t, k_t = tiles(q), tiles(k)
    pre = tiles(lf)
    tot = tiles(lf)

    row = lax.broadcasted_iota(jnp.int32, (SUBLANES, LANES), 0)
    m = 1
    while m < SUBLANES:
        right = (row & m) != 0
        qt, kt = [], []
        for i in range(nt):
            e = jnp.exp(jnp.where(right, pre[i], tot[i] - pre[i]))
            qt.append(jnp.where(right, q_t[i] * e, 0.0))
            kt.append(jnp.where(right, 0.0, k_t[i] * e))
            down = pltpu.roll(tot[i], m, 0)
            up = pltpu.roll(tot[i], SUBLANES - m, 0)
            pre[i] = pre[i] + jnp.where(right, down, 0.0)
            tot[i] = tot[i] + jnp.where(right, down, up)
        att = att + jnp.where(xor < 2 * m, nt_dot(cat(qt), cat(kt)), 0.0)
        m *= 2

    zero = jnp.zeros((SUBLANES, LANES), F32)
    mv = 1
    while mv < nt:
        qt, kt = [zero] * nt, [zero] * nt
        for base in range(0, nt, 2 * mv):
            for j in range(mv):
                lt, rt = base + j, base + mv + j
                kt[lt] = k_t[lt] * jnp.exp(tot[lt] - pre[lt])
                qt[rt] = q_t[rt] * jnp.exp(pre[rt])
        a = nt_dot(cat(qt), cat(kt))
        att = att + (a if 2 * mv == nt else jnp.where(xor < 2 * mv * SUBLANES, a, 0.0))
        for base in range(0, nt, 2 * mv):
            left_total = tot[base]
            both = left_total + tot[base + mv]
            for j in range(mv):
                pre[base + mv + j] = pre[base + mv + j] + left_total
            for j in range(2 * mv):
                tot[base + j] = both
        mv *= 2

    vb = v.astype(BF16)
    qg = cat([q_t[i] * jnp.exp(pre[i]) for i in range(nt)])
    kh = cat([k_t[i] * jnp.exp(tot[i] - pre[i]) for i in range(nt)])
    o = nt_dot(qg, s_t.astype(BF16)) + _dot(att.astype(BF16), vb)
    s_new = s_t * jnp.exp(tot[0][0:1, :]) + lax.dot_general(vb, kh, _TN, preferred_element_type=F32)
    return o, s_new


def _hgrn_rec_kernel(n_chunks, nh, q_ref, k_ref, lf_ref, v_ref, s0_ref, gn_ref, o_ref, s_out_ref, st_ref):
    t = pl.program_id(2)

    @pl.when(t == 0)
    def _():
        for h in range(nh):
            st_ref[h] = s0_ref[0, h].T

    def body(ci, carry):
        rows = pl.ds(pl.multiple_of(ci * REC_CHUNK, REC_CHUNK), REC_CHUNK)
        for h in range(nh):
            cols = slice(h * A_DK, (h + 1) * A_DK)
            o, s_new = _rec_chunk(q_ref[rows, cols], k_ref[rows, cols], v_ref[rows, cols],
                                  lf_ref[rows, cols], st_ref[h])
            st_ref[h] = s_new
            o_ref[rows, cols] = _rms_rows(o, gn_ref[...])
        return carry

    lax.fori_loop(0, n_chunks, body, 0)

    @pl.when(t == pl.num_programs(2) - 1)
    def _():
        for h in range(nh):
            s_out_ref[0, h] = st_ref[h].T


def _hgrn_rec(q, k, lf, v, s0, g_norm, bsz, seq, tb, nh):
    n_t = seq // tb
    blk = pl.BlockSpec((tb, nh * A_DK), lambda b, h, t: (b * n_t + t, h))
    st = pl.BlockSpec((1, nh, A_DK, A_DV), lambda b, h, t: (b, h, 0, 0))
    return pl.pallas_call(
        functools.partial(_hgrn_rec_kernel, tb // REC_CHUNK, nh),
        grid=(bsz, A_HEADS // nh, n_t),
        in_specs=[blk, blk, blk, blk, st, pl.BlockSpec((1, A_DV), lambda b, h, t: (0, 0))],
        out_specs=[blk, st],
        out_shape=[jax.ShapeDtypeStruct(q.shape, F32), jax.ShapeDtypeStruct(s0.shape, F32)],
        scratch_shapes=[pltpu.VMEM((nh, A_DV, A_DK), F32)],
        compiler_params=_cparams(("parallel", "parallel", "arbitrary")),
        name="hgrn_rec",
    )(q, k, lf, v, s0, g_norm.reshape(1, A_DV))


def _hgrn_step_kernel(nb, qt_ref, kt_ref, lft_ref, v_ref, s_ref, gn_ref, o_ref, s_out_ref):
    qt = qt_ref[0]
    kt = kt_ref[0]
    ft = jnp.exp(lft_ref[0])
    for b in range(nb):
        col = lambda a: jnp.broadcast_to(a[:, b:b + 1], (A_DK, A_DV))
        s_new = col(ft) * s_ref[b, 0] + col(kt) * v_ref[b:b + 1, :]
        s_out_ref[b, 0] = s_new
        o = jnp.sum(col(qt) * s_new, axis=0, keepdims=True)
        o_ref[b:b + 1, :] = _rms_rows(o, gn_ref[...])


def _hgrn_step(q, k, lf, v, s, g_norm):
    nb = q.shape[0]
    tr = lambda a: a.reshape(nb, A_HEADS, A_DK).transpose(1, 2, 0)
    colspec = pl.BlockSpec((1, A_DK, nb), lambda h: (h, 0, 0))
    vspec = pl.BlockSpec((nb, A_DV), lambda h: (0, h))
    sspec = pl.BlockSpec((nb, 1, A_DK, A_DV), lambda h: (0, h, 0, 0))
    return pl.pallas_call(
        functools.partial(_hgrn_step_kernel, nb),
        grid=(A_HEADS,),
        in_specs=[colspec, colspec, colspec, vspec, sspec, pl.BlockSpec((1, A_DV), lambda h: (0, 0))],
        out_specs=[vspec, sspec],
        out_shape=[jax.ShapeDtypeStruct(v.shape, F32), jax.ShapeDtypeStruct(s.shape, F32)],
        compiler_params=_cparams(("parallel",)),
        name="hgrn_step",
    )(tr(q), tr(k), tr(lf), v, s, g_norm.reshape(1, A_DV))


def _out_proj_kernel(a_ref, sg_ref, x_ref, w_ref, y_ref):
    y_ref[...] = x_ref[...] + _dot((a_ref[...] * sg_ref[...]).astype(BF16), w_ref[...])


def _out_proj(a, sg, x, w_bf, tm):
    m = x.shape[0]
    row = pl.BlockSpec((tm, D_MODEL), lambda i: (i, 0))
    return pl.pallas_call(
        _out_proj_kernel,
        grid=(m // tm,),
        in_specs=[row, row, row, pl.BlockSpec(w_bf.shape, lambda i: (0, 0))],
        out_specs=row,
        out_shape=jax.ShapeDtypeStruct(x.shape, F32),
        compiler_params=_cparams(("parallel",)),
        name="out_proj",
    )(a, sg, x, w_bf)


BIAS_COLS = 6


def _bias_lane(h):
    return (B_DH if h % 2 == 0 else 0) + BIAS_COLS * (h // 2)


def _aug_constants():
    place_q = np.zeros((3, B_HEADS, LANES), np.float32)
    place_k = np.zeros((3, B_HEADS, LANES), np.float32)
    ones_q = np.zeros((1, LANES), np.float32)
    ones_k = np.zeros((1, LANES), np.float32)
    for h in range(B_HEADS):
        base = _bias_lane(h)
        for p in range(3):
            place_q[p, h, base + p] = 1.0
            ones_q[0, base + 3 + p] = 1.0
            ones_k[0, base + p] = 1.0
            place_k[p, h, base + 3 + p] = -1.0
    as_bf = lambda a: jnp.asarray(a, BF16)
    return as_bf(place_q), jnp.asarray(ones_q), as_bf(place_k), jnp.asarray(ones_k)


def _bias_columns(pieces, place_ref, ones_ref):
    out = ones_ref[...]
    for p in range(3):
        out = out + _dot(pieces[p], place_ref[p])
    return out


def _store_heads(x, other, o_ref):
    lane = lax.broadcasted_iota(jnp.int32, (x.shape[0], LANES), 1)
    for h in range(B_HEADS):
        own = (lane < B_DH) if h % 2 == 0 else (lane >= B_DH)
        xv = x[:, (h // 2) * LANES:(h // 2 + 1) * LANES]
        o_ref[0, h] = jnp.where(own, xv, other(h, lane)).astype(BF16)


def _shared_kv_kernel(prompt, x_ref, g_ref, wkv_ref, kn_ref, wf_ref, bf_ref, ind_ref, indt_ref,
                      place_ref, ones_ref, k_ref, v_ref, lf_ref, *rest):
    tm = x_ref.shape[0]
    u = _rms_rows(x_ref[...], g_ref[...]).astype(BF16)
    bw = B_HEADS * B_DH
    k = _head_rms(_dot(u, wkv_ref[:, 0:bw]), ind_ref[...], indt_ref[...], kn_ref[...])
    v = _dot(u, wkv_ref[:, bw:])
    z = _dot(u, wf_ref[...]) + bf_ref[...]
    lf = jnp.minimum(z, 0.0) - jnp.log(1.0 + jnp.exp(-jnp.abs(z)))
    lf_ref[...] = lf
    if not prompt:
        k_ref[...] = k
        v_ref[...] = v
        return

    chi_ref, cmid_ref, clo_ref, kaug_ref, vtaug_ref, carry_ref = rest
    tail = jnp.where(lax.broadcasted_iota(jnp.int32, (VT_ROWS - B_DH, tm), 0) == 0, 1.0, 0.0).astype(BF16)
    for j in range(B_HEADS // 2):
        cols = slice(j * LANES, (j + 1) * LANES)
        k_ref[0, cols, :] = k[:, cols].T
        v_pair = v[:, cols].T
        v_ref[0, cols, :] = v_pair
        for half in range(2):
            vtaug_ref[0, 2 * j + half, 0:B_DH, :] = v_pair[half * B_DH:(half + 1) * B_DH, :].astype(BF16)
            vtaug_ref[0, 2 * j + half, B_DH:VT_ROWS, :] = tail

    @pl.when(pl.program_id(1) == 0)
    def _():
        carry_ref[...] = jnp.zeros_like(carry_ref)

    lower = (lax.broadcasted_iota(jnp.int32, (tm, tm), 0)
             >= lax.broadcasted_iota(jnp.int32, (tm, tm), 1)).astype(BF16)
    c = _sel_dot3(lower, lf) + carry_ref[...]
    carry_ref[...] = c[tm - 1:tm, :]
    pieces = _split3(c * LOG2E)
    chi_ref[...], cmid_ref[...], clo_ref[...] = pieces
    bias = _bias_columns(pieces, place_ref, ones_ref)

    lane_row = lax.broadcasted_iota(jnp.int32, (1, LANES), 1)

    def own_bias(h, lane):
        offset = lane_row - _bias_lane(h)
        return bias * jnp.where((offset >= 0) & (offset < BIAS_COLS), 1.0, 0.0)

    _store_heads(k, own_bias, kaug_ref)


def _head_indicator():
    ind = (jnp.arange(B_HEADS * B_DH)[:, None] // B_DH == jnp.arange(B_HEADS)[None, :])
    return ind.astype(BF16), ind.T.astype(BF16)


def _shared_kv(x, g, wkv_bf, k_norm, wf_bf, b_fgate, bsz, seq, tm, prompt):
    m = x.shape[0]
    n_t = seq // tm
    bw = B_HEADS * B_DH
    ind, ind_t = _head_indicator()
    _, _, place_k, ones_k = _aug_constants()
    row = pl.BlockSpec((tm, D_MODEL), lambda b, t: (b * n_t + t, 0))
    hrow = pl.BlockSpec((tm, B_HEADS), lambda b, t: (b * n_t + t, 0))
    full = lambda a: pl.BlockSpec(a.shape, lambda b, t: (0,) * a.ndim)
    consts = [g.reshape(1, D_MODEL), wkv_bf, jnp.tile(k_norm, B_HEADS).reshape(1, D_MODEL),
              wf_bf, b_fgate.reshape(1, B_HEADS), ind, ind_t, place_k, ones_k]
    small = jax.ShapeDtypeStruct((m, B_HEADS), F32)
    if prompt:
        cols = pl.BlockSpec((1, bw, tm), lambda b, t: (b, 0, t))
        kv_out = jax.ShapeDtypeStruct((bsz, bw, seq), F32)
        piece = jax.ShapeDtypeStruct((m, B_HEADS), BF16)
        out_specs = [cols, cols, hrow, hrow, hrow, hrow,
                     pl.BlockSpec((1, B_HEADS, tm, LANES), lambda b, t: (b, 0, t, 0)),
                     pl.BlockSpec((1, B_HEADS, VT_ROWS, tm), lambda b, t: (b, 0, 0, t))]
        out_shape = [kv_out, kv_out, small, piece, piece, piece,
                     jax.ShapeDtypeStruct((bsz, B_HEADS, seq, LANES), BF16),
                     jax.ShapeDtypeStruct((bsz, B_HEADS, VT_ROWS, seq), BF16)]
        scratch = [pltpu.VMEM((1, B_HEADS), F32)]
    else:
        out_specs = [row, row, hrow]
        out_shape = [jax.ShapeDtypeStruct((m, D_MODEL), F32)] * 2 + [small]
        scratch = []
    return pl.pallas_call(
        functools.partial(_shared_kv_kernel, prompt),
        grid=(bsz, n_t),
        in_specs=[row] + [full(a) for a in consts],
        out_specs=out_specs,
        out_shape=out_shape,
        scratch_shapes=scratch,
        compiler_params=_cparams(("parallel", "arbitrary")),
        name="shared_kv",
    )(x, *consts)


def _fox_qg(x_ref, g_ref, w_ref, qn_ref, ind_ref, indt_ref):
    xn = _rms_rows(x_ref[...], g_ref[...]).astype(BF16)
    bw = B_HEADS * B_DH
    q = _head_rms(_dot(xn, w_ref[:, 0:bw]), ind_ref[...], indt_ref[...], qn_ref[...])
    pg = _dot(xn, w_ref[:, bw:])
    return q, pg * _sigmoid(pg)


def _fox_proj_kernel(x_ref, g_ref, w_ref, qn_ref, ind_ref, indt_ref, q_ref, sg_ref):
    q, sg = _fox_qg(x_ref, g_ref, w_ref, qn_ref, ind_ref, indt_ref)
    q_ref[...] = q * ATTN_SCALE
    sg_ref[...] = sg


def _fox_proj_aug_kernel(x_ref, g_ref, w_ref, qn_ref, ind_ref, indt_ref, chi_ref, cmid_ref, clo_ref,
                         place_ref, ones_ref, qaug_ref, sg_ref):
    q, sg = _fox_qg(x_ref, g_ref, w_ref, qn_ref, ind_ref, indt_ref)
    sg_ref[...] = sg
    pieces = (chi_ref[...], cmid_ref[...], clo_ref[...])
    bias = _bias_columns(pieces, place_ref, ones_ref)
    _store_heads(q * (ATTN_SCALE * LOG2E), lambda h, lane: bias, qaug_ref)


def _fox_proj(x, g, w_bf, q_norm, tm):
    m = x.shape[0]
    ind, ind_t = _head_indicator()
    row = pl.BlockSpec((tm, D_MODEL), lambda i: (i, 0))
    full = lambda a: pl.BlockSpec(a.shape, lambda i: (0,) * a.ndim)
    consts = [g.reshape(1, D_MODEL), w_bf, jnp.tile(q_norm, B_HEADS).reshape(1, D_MODEL), ind, ind_t]
    out = jax.ShapeDtypeStruct((m, D_MODEL), F32)
    return pl.pallas_call(
        _fox_proj_kernel,
        grid=(m // tm,),
        in_specs=[row] + [full(a) for a in consts],
        out_specs=[row, row],
        out_shape=[out, out],
        compiler_params=_cparams(("parallel",)),
        name="fox_proj",
    )(x, *consts)


def _fox_proj_aug(x, g, w_bf, q_norm, c_pieces, bsz, seq, tm):
    m = x.shape[0]
    n_t = seq // tm
    ind, ind_t = _head_indicator()
    place_q, ones_q, _, _ = _aug_constants()
    row = pl.BlockSpec((tm, D_MODEL), lambda b, t: (b * n_t + t, 0))
    hrow = pl.BlockSpec((tm, B_HEADS), lambda b, t: (b * n_t + t, 0))
    heads = pl.BlockSpec((1, B_HEADS, tm, LANES), lambda b, t: (b, 0, t, 0))
    full = lambda a: pl.BlockSpec(a.shape, lambda b, t: (0,) * a.ndim)
    consts = [g.reshape(1, D_MODEL), w_bf, jnp.tile(q_norm, B_HEADS).reshape(1, D_MODEL), ind, ind_t]
    return pl.pallas_call(
        _fox_proj_aug_kernel,
        grid=(bsz, n_t),
        in_specs=[row] + [full(a) for a in consts] + [hrow] * 3 + [full(place_q), full(ones_q)],
        out_specs=[heads, row],
        out_shape=[jax.ShapeDtypeStruct((bsz, B_HEADS, seq, LANES), BF16),
                   jax.ShapeDtypeStruct((m, D_MODEL), F32)],
        compiler_params=_cparams(("parallel", "parallel")),
        name="fox_proj_aug",
    )(x, *consts, *c_pieces, place_q, ones_q)


def _flash_kernel(tq, q_ref, k_ref, vt_ref, o_ref, s_ref, m_ref, acc_ref):
    qi = pl.program_id(2)

    def block(ki):
        return pl.ds(pl.multiple_of(ki * tq, tq), tq)

    def put_scores(slot, ki):
        for h in range(FLASH_HEADS):
            s_ref[slot, h] = lax.dot_general(k_ref[0, h, block(ki), :], q_ref[0, h], _NT,
                                             preferred_element_type=F32)

    def consume(slot, ki, diagonal):
        for h in range(FLASH_HEADS):
            vt = vt_ref[0, h, :, block(ki)]
            for c in range(tq // MXU_WIDTH):
                cols = slice(c * MXU_WIDTH, (c + 1) * MXU_WIDTH)
                st = s_ref[slot, h, :, cols]
                if diagonal:
                    keep = (lax.broadcasted_iota(jnp.int32, (tq, MXU_WIDTH), 0)
                            <= lax.broadcasted_iota(jnp.int32, (tq, MXU_WIDTH), 1) + c * MXU_WIDTH)
                    st = jnp.where(keep, st, -jnp.inf)
                m = m_ref[h, :, cols]
                m_new = jnp.maximum(m, jnp.max(st, axis=0, keepdims=True))
                p = jnp.exp2(st - m_new).astype(BF16)
                acc_ref[h, :, cols] = acc_ref[h, :, cols] * jnp.exp2(m - m_new) + _dot(vt, p)
                m_ref[h, :, cols] = m_new

    m_ref[...] = jnp.full(m_ref.shape, -jnp.inf, F32)
    acc_ref[...] = jnp.zeros(acc_ref.shape, F32)
    put_scores(0, 0)

    def pair(ki):
        put_scores(1, ki + 1)
        consume(0, ki, False)
        put_scores(0, ki + 2)
        consume(1, ki + 1, False)

    def body(j, carry):
        pair(4 * j)
        pair(4 * j + 2)
        return carry

    lax.fori_loop(0, qi // 4, body, 0)

    @pl.when(qi % 4 >= 2)
    def _():
        pair(4 * (qi // 4))

    @pl.when(qi % 2 == 0)
    def _():
        consume(0, qi, True)

    @pl.when(qi % 2 == 1)
    def _():
        put_scores(1, qi)
        consume(0, qi - 1, False)
        consume(1, qi, True)

    outs = [acc_ref[h, 0:B_DH, :] / acc_ref[h, B_DH:B_DH + 1, :] for h in range(FLASH_HEADS)]
    o_ref[...] = jnp.concatenate(outs, axis=0).T


def _flash(q_aug, k_aug, vt_aug, tq):
    bsz, _, seq, _ = q_aug.shape
    n_q = seq // tq
    qspec = pl.BlockSpec((1, FLASH_HEADS, tq, LANES), lambda b, hp, i: (b, hp, i, 0))
    kspec = pl.BlockSpec((1, FLASH_HEADS, seq, LANES), lambda b, hp, i: (b, hp, 0, 0))
    vspec = pl.BlockSpec((1, FLASH_HEADS, VT_ROWS, seq), lambda b, hp, i: (b, hp, 0, 0))
    return pl.pallas_call(
        functools.partial(_flash_kernel, tq),
        grid=(bsz, B_HEADS // FLASH_HEADS, n_q),
        in_specs=[qspec, kspec, vspec],
        out_specs=pl.BlockSpec((tq, FLASH_HEADS * B_DH), lambda b, hp, i: (b * n_q + i, hp)),
        out_shape=jax.ShapeDtypeStruct((bsz * seq, B_HEADS * B_DH), F32),
        scratch_shapes=[pltpu.VMEM((2, FLASH_HEADS, tq, tq), F32), pltpu.VMEM((FLASH_HEADS, 1, tq), F32),
                        pltpu.VMEM((FLASH_HEADS, VT_ROWS, tq), F32)],
        compiler_params=_cparams(("parallel", "parallel", "arbitrary")),
        name="fox_flash",
    )(q_aug, k_aug, vt_aug)


def _decode_kernel(npp, pt_ref, qcol_ref, knew_ref, vnew_ref, lfnew_ref, indt_ref, *rest):
    kt_refs = rest[0:npp]
    vt_refs = rest[npp:2 * npp]
    lf_refs = rest[2 * npp:3 * npp]
    o_ref = rest[3 * npp]
    m_ref, l_ref, acc_ref, carry_ref = rest[3 * npp + 1:]
    j = pl.program_id(1)
    page = LANES
    later = (lax.broadcasted_iota(jnp.int32, (page, page), 0)
             > lax.broadcasted_iota(jnp.int32, (page, page), 1)).astype(BF16)

    def update(pages, only_first_lane):
        n = len(pages)
        lf_all = jnp.concatenate([lf for _, _, lf in pages], axis=0)
        sfx_all = _dot3(lf_all, later)
        carry = carry_ref[...]
        logits = []
        for i, (kt, _, lf) in enumerate(pages):
            rows = [jnp.sum(kt[h] * qcol_ref[0, h], axis=0, keepdims=True) for h in range(B_HEADS)]
            lg = jnp.concatenate(rows, axis=0) + sfx_all[i * B_HEADS:(i + 1) * B_HEADS] + carry
            if only_first_lane:
                lane = lax.broadcasted_iota(jnp.int32, (B_HEADS, page), 1)
                lg = jnp.where(lane == 0, lg, -jnp.inf)
            logits.append(lg)
            carry = carry + jnp.sum(lf, axis=-1, keepdims=True)
        carry_ref[...] = carry
        m_old = m_ref[...]
        m_new = m_old
        for lg in logits:
            m_new = jnp.maximum(m_new, jnp.max(lg, axis=-1, keepdims=True))
        alpha = jnp.exp(m_old - m_new)
        probs = [jnp.exp(lg - m_new) for lg in logits]
        l_new = l_ref[...] * alpha
        for p in probs:
            l_new = l_new + p
        l_ref[...] = l_new
        m_ref[...] = m_new
        for h in range(B_HEADS):
            a = acc_ref[h] * alpha[h:h + 1, :]
            for i in range(n):
                a = a + probs[i][h:h + 1, :] * pages[i][1][h]
            acc_ref[h] = a

    @pl.when(j == 0)
    def _():
        m_ref[...] = jnp.full(m_ref.shape, -jnp.inf, F32)
        l_ref[...] = jnp.zeros(l_ref.shape, F32)
        acc_ref[...] = jnp.zeros(acc_ref.shape, F32)
        carry_ref[...] = jnp.zeros(carry_ref.shape, F32)
        update([(knew_ref.at[0], vnew_ref.at[0], lfnew_ref[0])], True)

    update([(kt_refs[i].at[0], vt_refs[i].at[0], lf_refs[i][0]) for i in range(npp)], False)

    @pl.when(j == pl.num_programs(1) - 1)
    def _():
        ones = jnp.ones((SUBLANES, page), BF16)
        lane_sum = lambda x: sum(lax.dot_general(ones, piece, _NT, preferred_element_type=F32)
                                 for piece in _split3(x))[0:1, :]
        num = lane_sum(acc_ref[...].reshape(B_HEADS * B_DH, page))
        den = lane_sum(l_ref[...])
        o_ref[0] = num / _dot3(den, indt_ref[...])


def _positions_on_lanes(a, nb, page):
    a = a.reshape(nb, B_HEADS, -1, 1)
    return jnp.pad(a, ((0, 0), (0, 0), (0, 0), (0, page - 1)))


def _fox_attention_sample(qs, k_new, v_new, lf_new, cache_kt, cache_vt, cache_lft, page_table, npp):
    nb, n_pages = page_table.shape
    page = cache_kt.shape[-1]
    bw = B_HEADS * B_DH
    _, ind_t = _head_indicator()
    n_steps = n_pages // npp
    qcol = jnp.broadcast_to(qs.reshape(nb, B_HEADS, B_DH, 1), (nb, B_HEADS, B_DH, page))
    knew = _positions_on_lanes(k_new, nb, page)
    vnew = _positions_on_lanes(v_new, nb, page)
    lfnew = _positions_on_lanes(lf_new, nb, page).reshape(nb, B_HEADS, page)

    def page_map(i, ndim):
        def index_map(b, j, pt):
            return (pt[b * n_pages + (n_pages - 1 - (j * npp + i))],) + (0,) * (ndim - 1)
        return index_map

    tile = (1, B_HEADS, B_DH, page)
    per_seq_tile = pl.BlockSpec(tile, lambda b, j, pt: (b, 0, 0, 0))
    in_specs = [per_seq_tile, per_seq_tile, per_seq_tile,
                pl.BlockSpec((1, B_HEADS, page), lambda b, j, pt: (b, 0, 0)),
                pl.BlockSpec(ind_t.shape, lambda b, j, pt: (0, 0))]
    in_specs += [pl.BlockSpec(tile, page_map(i, 4)) for i in range(npp)]
    in_specs += [pl.BlockSpec(tile, page_map(i, 4)) for i in range(npp)]
    in_specs += [pl.BlockSpec((1, B_HEADS, page), page_map(i, 3)) for i in range(npp)]
    grid_spec = pltpu.PrefetchScalarGridSpec(
        num_scalar_prefetch=1,
        grid=(nb, n_steps),
        in_specs=in_specs,
        out_specs=pl.BlockSpec((1, 1, bw), lambda b, j, pt: (b, 0, 0)),
        scratch_shapes=[pltpu.VMEM((B_HEADS, 1), F32), pltpu.VMEM((B_HEADS, page), F32),
                        pltpu.VMEM((B_HEADS, B_DH, page), F32), pltpu.VMEM((B_HEADS, 1), F32)],
    )
    o = pl.pallas_call(
        functools.partial(_decode_kernel, npp),
        grid_spec=grid_spec,
        out_shape=jax.ShapeDtypeStruct((nb, 1, bw), F32),
        compiler_params=_cparams(("parallel", "arbitrary")),
        name="fox_decode",
    )(page_table.reshape(-1), qcol, knew, vnew, lfnew, ind_t,
      *([cache_kt] * npp), *([cache_vt] * npp), *([cache_lft] * npp))
    return o.reshape(nb, bw)


def _pick(n, pref):
    return pref if n % pref == 0 else n


def kernel(x_prompt, x_sample, state_hgrn, cache_k, cache_v, cache_logf, page_table,
           norm_in, w_in_a, lb_logits, g_norm_a, w_o_a, norm_kv, w_kv, k_norm, w_fgate, b_fgate,
           w_in_b, q_norm_b, w_o_b):
    bsz, seq, _ = x_prompt.shape
    nb = x_sample.shape[0]
    n_a = w_in_a.shape[0]
    n_b = w_in_b.shape[0]
    tm = _pick(seq, 512)
    tb = _pick(seq, 512)
    tq = _pick(seq, 512)
    npp = _pick(page_table.shape[1], 16)

    hp = x_prompt.reshape(bsz * seq, D_MODEL)
    hs = x_sample.reshape(nb, D_MODEL)
    zero_state = jnp.zeros((bsz, A_HEADS, A_DK, A_DV), F32)
    sp, ss = [], []
    for l in range(n_a):
        w_in = w_in_a[l].astype(BF16)
        w_o = w_o_a[l].astype(BF16)
        q, k, lf, v, sg = _hgrn_proj(hp, norm_in[l], w_in, lb_logits, l, tm)
        o, s_new = _hgrn_rec(q, k, lf, v, zero_state, g_norm_a[l], bsz, seq, tb, REC_HEADS)
        hp = _out_proj(o, sg, hp, w_o, tm)
        sp.append(s_new)
        q, k, lf, v, sg = _hgrn_proj(hs, norm_in[l], w_in, lb_logits, l, nb)
        o, s_new = _hgrn_step(q, k, lf, v, state_hgrn[l], g_norm_a[l])
        hs = _out_proj(o, sg, hs, w_o, nb)
        ss.append(s_new)

    wkv = w_kv.astype(BF16)
    wf = w_fgate.astype(BF16)
    kpt, vpt, lfp, *c_pieces, k_aug, vt_aug = _shared_kv(hp, norm_kv, wkv, k_norm, wf, b_fgate,
                                                        bsz, seq, tm, True)
    heads_last = lambda a: jnp.transpose(a.reshape(bsz, B_HEADS, B_DH, seq), (0, 3, 1, 2))
    ks, vs, lfs = _shared_kv(hs, norm_kv, wkv, k_norm, wf, b_fgate, 1, nb, nb, False)

    ck = jnp.transpose(cache_k, (0, 2, 3, 1))
    cv = jnp.transpose(cache_v, (0, 2, 3, 1))
    clf = jnp.transpose(cache_logf, (0, 2, 1))
    for j in range(n_b):
        w_in = w_in_b[j].astype(BF16)
        w_o = w_o_b[j].astype(BF16)
        g = norm_in[n_a + j]
        q_aug, sg = _fox_proj_aug(hp, g, w_in, q_norm_b[j], c_pieces, bsz, seq, tm)
        o = _flash(q_aug, k_aug, vt_aug, tq)
        hp = _out_proj(o, sg, hp, w_o, tm)
        qs, sg = _fox_proj(hs, g, w_in, q_norm_b[j], nb)
        o = _fox_attention_sample(qs, ks, vs, lfs, ck, cv, clf, page_table, npp)
        hs = _out_proj(o, sg, hs, w_o, nb)

    return (hp.reshape(bsz, seq, D_MODEL), hs.reshape(nb, 1, D_MODEL),
            jnp.stack(sp), jnp.stack(ss),
            heads_last(kpt), ks.reshape(nb, 1, B_HEADS, B_DH),
            heads_last(vpt), vs.reshape(nb, 1, B_HEADS, B_DH),
            lfp.reshape(bsz, seq, B_HEADS), lfs.reshape(nb, 1, B_HEADS))
```
